```python
import math
import jax, jax.numpy as jnp
from jax import lax
import numpy as np

D_MODEL = 1024
BATCH = 16
SEQ = 2048
DEPTH = 4
DEC_BATCH = 16
DEC_SEQ = 4096
PAST_LEN = 128

GLA_HEADS = 4
GLA_DK = 64
GLA_DV = 128
GLA_RANK = 16
GLA_GATE_NORM = 16.0
DIFF_HEADS = 4
DIFF_DH = 64
DIFF_DV = 2 * DIFF_DH
Q_BLOCK = 128
ROPE_THETA = 500000.0
ROPE_DIM = DIFF_DH // 4
SSD_HEADS = 8
SSD_P = 64
SSD_GROUPS = 2
SSD_E = SSD_HEADS // SSD_GROUPS
SSD_N = 64
SSD_CONV = 5
CHUNK = 64
D_FF = 4 * D_MODEL
EPS = 1e-6

GLA_QK = GLA_HEADS * GLA_DK
GLA_V = GLA_HEADS * GLA_DV
DIFF_QK = DIFF_HEADS * 2 * DIFF_DH
DIFF_V = DIFF_HEADS * DIFF_DV
SSD_INNER = SSD_HEADS * SSD_P
SSD_BC = SSD_GROUPS * SSD_N
SSD_XBC = SSD_INNER + 2 * SSD_BC
MIX_WIDTH = GLA_V + DIFF_V + SSD_INNER
IN_SIZES = (GLA_QK, GLA_QK, GLA_V, GLA_V, 2 * GLA_RANK,
            DIFF_QK, DIFF_QK, DIFF_V,
            SSD_INNER, SSD_XBC, 2 * SSD_HEADS)
IN_WIDTH = sum(IN_SIZES)
IN_OFFSETS = tuple(int(v) for v in np.cumsum(IN_SIZES)[:-1])

kernel_name = "hybrid_bidir_gla_diff_ssd_encoder"


def rmsnorm(x, g):
    xf = x.astype(jnp.float32)
    y = xf * lax.rsqrt(jnp.mean(xf * xf, axis=-1, keepdims=True) + EPS)
    return (y * g.astype(jnp.float32)).astype(x.dtype)


def flip(t):
    return jnp.flip(t, axis=1)


def to_chunks(t):
    b, s = t.shape[:2]
    return t.reshape(b, s // CHUNK, CHUNK, *t.shape[2:]).swapaxes(0, 1)


def from_chunks(t):
    nc, b, c = t.shape[:3]
    return t.swapaxes(0, 1).reshape(b, nc * c, *t.shape[3:])


def rope_tables(s):
    inv = ROPE_THETA ** (-jnp.arange(0, ROPE_DIM, 2, dtype=jnp.float32) / ROPE_DIM)
    ang = jnp.arange(s, dtype=jnp.float32)[:, None] * inv[None, :]
    ang = jnp.concatenate([ang, ang], axis=-1)
    return jnp.cos(ang), jnp.sin(ang)


def apply_partial_rope(t, cos, sin):
    c = cos[:, None, None, :].astype(t.dtype)
    s = sin[:, None, None, :].astype(t.dtype)
    tr, tp = t[..., :ROPE_DIM], t[..., ROPE_DIM:]
    t1, t2 = tr[..., :ROPE_DIM // 2], tr[..., ROPE_DIM // 2:]
    rot = jnp.concatenate([-t2, t1], axis=-1)
    return jnp.concatenate([tr * c + rot * s, tp], axis=-1)


def gla_scan(q, k, v, logg):
    q, k, v, logg = (t.astype(jnp.float32) for t in (q, k, v, logg))
    b_, _, h, dk = q.shape
    dv = v.shape[-1]
    mask = jnp.tril(jnp.ones((CHUNK, CHUNK), dtype=bool))

    def step(state, inp):
        qi, ki, vi, gi = inp
        b = jnp.cumsum(gi, axis=1)
        diff = b[:, :, None] - b[:, None, :]
        decay = jnp.exp(jnp.where(mask[:, :, None, None], diff, -jnp.inf))
        attn = jnp.einsum('bihd,bjhd,bijhd->bijh', qi, ki, decay)
        o = (jnp.einsum('bijh,bjhe->bihe', attn, vi)
             + jnp.einsum('bihd,bhde->bihe', qi * jnp.exp(b), state))
        b_last = b[:, -1]
        state = (state * jnp.exp(b_last)[..., None]
                 + jnp.einsum('bjhd,bjhe->bhde', ki * jnp.exp(b_last[:, None] - b), vi))
        return state, o

    state0 = jnp.zeros((b_, h, dk, dv), jnp.float32)
    _, o = lax.scan(step, state0, (to_chunks(q), to_chunks(k), to_chunks(v), to_chunks(logg)))
    return from_chunks(o)


def gla_mixer(gq, gk, gv, gg, glr, wg_f, bg_f, wg_b, bg_b, norm_g):
    b_, s, _ = gq.shape
    q = gq.reshape(b_, s, GLA_HEADS, GLA_DK) * (GLA_DK ** -0.5)
    k = gk.reshape(b_, s, GLA_HEADS, GLA_DK)
    v = gv.reshape(b_, s, GLA_HEADS, GLA_DV)
    r_f, r_b = glr[..., :GLA_RANK], glr[..., GLA_RANK:]
    logg_f = (jax.nn.log_sigmoid((r_f @ wg_f + bg_f).astype(jnp.float32)) / GLA_GATE_NORM
              ).reshape(b_, s, GLA_HEADS, GLA_DK)
    logg_b = (jax.nn.log_sigmoid((r_b @ wg_b + bg_b).astype(jnp.float32)) / GLA_GATE_NORM
              ).reshape(b_, s, GLA_HEADS, GLA_DK)
    o = gla_scan(q, k, v, logg_f) + flip(gla_scan(flip(q), flip(k), flip(v), flip(logg_b)))
    o = rmsnorm(o.astype(gv.dtype), norm_g)
    o = o * jax.nn.silu(gg.reshape(b_, s, GLA_HEADS, GLA_DV))
    return o.reshape(b_, s, GLA_V)


def diff_mixer(dq, dk, dvv, qn, kn, lq1, lk1, lq2, lk2, subln, lambda_init, cos, sin):
    b_, s, _ = dq.shape
    q = rmsnorm(dq.reshape(b_, s, DIFF_HEADS, 2, DIFF_DH), qn)
    k = rmsnorm(dk.reshape(b_, s, DIFF_HEADS, 2, DIFF_DH), kn)
    q = apply_partial_rope(q, cos, sin)
    k = apply_partial_rope(k, cos, sin)
    v = dvv.reshape(b_, s, DIFF_HEADS, DIFF_DV)
    lam = (jnp.exp(jnp.sum(lq1.astype(jnp.float32) * lk1.astype(jnp.float32)))
           - jnp.exp(jnp.sum(lq2.astype(jnp.float32) * lk2.astype(jnp.float32)))
           + lambda_init)
    scale = DIFF_DH ** -0.5
    nq = s // Q_BLOCK
    qb = q.reshape(b_, nq, Q_BLOCK, DIFF_HEADS, 2, DIFF_DH).transpose(1, 0, 3, 4, 2, 5)
    kt = k.transpose(0, 2, 3, 1, 4)
    vt = v.transpose(0, 2, 1, 3)

    def block(qblk):
        sc = jnp.einsum('bhcqd,bhckd->bhcqk', qblk, kt).astype(jnp.float32) * scale
        p = jax.nn.softmax(sc, axis=-1)
        w = p[:, :, 0] - lam * p[:, :, 1]
        return jnp.einsum('bhqk,bhkv->bhqv', w.astype(vt.dtype), vt)

    o = lax.map(block, qb)
    o = o.transpose(1, 0, 3, 2, 4).reshape(b_, s, DIFF_HEADS, DIFF_DV)
    o = rmsnorm(o, subln) * (1.0 - lambda_init)
    return o.reshape(b_, s, DIFF_V)


def ssd_scan(x, dt, a_neg, bm, cm):
    x, dt, bm, cm = (t.astype(jnp.float32) for t in (x, dt, bm, cm))
    la = dt * a_neg.astype(jnp.float32)
    b_ = x.shape[0]
    mask = jnp.tril(jnp.ones((CHUNK, CHUNK), dtype=bool))

    def step(state, inp):
        xi, dti, lai, bi, ci = inp
        acum = jnp.cumsum(lai, axis=1)
        seg = acum[:, :, None] - acum[:, None, :]
        lmat = jnp.exp(jnp.where(mask[:, :, None, None], seg, -jnp.inf))
        cb = jnp.einsum('bign,bjgn->bijg', ci, bi)
        m = cb[..., None] * lmat * dti[:, None]
        y = (jnp.einsum('bijge,bjgep->bigep', m, xi)
             + jnp.einsum('bign,bgepn->bigep', ci, state) * jnp.exp(acum)[..., None])
        w = jnp.exp(acum[:, -1:] - acum) * dti
        state = (state * jnp.exp(acum[:, -1])[..., None, None]
                 + jnp.einsum('bjge,bjgep,bjgn->bgepn', w, xi, bi))
        return state, y

    state0 = jnp.zeros((b_, SSD_GROUPS, SSD_E, SSD_P, SSD_N), jnp.float32)
    _, y = lax.scan(step, state0, (to_chunks(x), to_chunks(dt), to_chunks(la),
                                   to_chunks(bm), to_chunks(cm)))
    return from_chunks(y)


def ssd_mixer(sz, sxbc, sdt, conv_w, conv_b, dtb_f, dtb_b, alog_f, alog_b, d_skip, norm_g):
    b_, s, _ = sz.shape
    pad = SSD_CONV // 2
    xbc = lax.conv_general_dilated(sxbc, conv_w[:, None, :], window_strides=(1,),
                                   padding=[(pad, pad)],
                                   dimension_numbers=('NWC', 'WIO', 'NWC'),
                                   feature_group_count=SSD_XBC)
    xbc = jax.nn.silu(xbc + conv_b)
    xs = xbc[..., :SSD_INNER].reshape(b_, s, SSD_GROUPS, SSD_E, SSD_P)
    bm = xbc[..., SSD_INNER:SSD_INNER + SSD_BC].reshape(b_, s, SSD_GROUPS, SSD_N)
    cm = xbc[..., SSD_INNER + SSD_BC:].reshape(b_, s, SSD_GROUPS, SSD_N)
    dt_f = jax.nn.softplus(sdt[..., :SSD_HEADS] + dtb_f).reshape(b_, s, SSD_GROUPS, SSD_E)
    dt_b = jax.nn.softplus(sdt[..., SSD_HEADS:] + dtb_b).reshape(b_, s, SSD_GROUPS, SSD_E)
    a_f = -jnp.exp(alog_f).reshape(SSD_GROUPS, SSD_E)
    a_b = -jnp.exp(alog_b).reshape(SSD_GROUPS, SSD_E)
    y = (ssd_scan(xs, dt_f, a_f, bm, cm)
         + flip(ssd_scan(flip(xs), flip(dt_b), a_b, flip(bm), flip(cm)))).astype(xs.dtype)
    y = y + d_skip.reshape(SSD_GROUPS, SSD_E)[:, :, None] * xs
    y = y.reshape(b_, s, SSD_INNER)
    return rmsnorm(y * jax.nn.silu(sz), norm_g)


def trunk(x, norm1, w_in, gla_wg_f, gla_bg_f, gla_wg_b, gla_bg_b, gla_norm,
          diff_qnorm, diff_knorm, diff_lq1, diff_lk1, diff_lq2, diff_lk2, diff_subln,
          ssd_conv_w, ssd_conv_b, ssd_dt_bias_f, ssd_dt_bias_b, ssd_A_log_f, ssd_A_log_b,
          ssd_D, ssd_norm, w_out, norm2, w_mlp1, w_mlp2):
    s = x.shape[1]
    cos, sin = rope_tables(s)
    for l in range(DEPTH):
        h = rmsnorm(x, norm1[l])
        (gq, gk, gv, gg, glr, dq, dk, dvv, sz, sxbc, sdt) = jnp.split(h @ w_in[l], IN_OFFSETS, axis=-1)
        o_gla = gla_mixer(gq, gk, gv, gg, glr, gla_wg_f[l], gla_bg_f[l], gla_wg_b[l], gla_bg_b[l],
                          gla_norm[l])
        lambda_init = 0.8 - 0.6 * math.exp(-0.3 * l)
        o_diff = diff_mixer(dq, dk, dvv, diff_qnorm[l], diff_knorm[l], diff_lq1[l], diff_lk1[l],
                            diff_lq2[l], diff_lk2[l], diff_subln[l], lambda_init, cos, sin)
        o_ssd = ssd_mixer(sz, sxbc, sdt, ssd_conv_w[l], ssd_conv_b[l], ssd_dt_bias_f[l],
                          ssd_dt_bias_b[l], ssd_A_log_f[l], ssd_A_log_b[l], ssd_D[l], ssd_norm[l])
        x = x + jnp.concatenate([o_gla, o_diff, o_ssd], axis=-1) @ w_out[l]
        h = rmsnorm(x, norm2[l])
        x = x + jnp.square(jax.nn.relu(h @ w_mlp1[l])) @ w_mlp2[l]
    return x


def setup_inputs(seed: int = 0) -> dict:
    key = jax.random.key(seed)
    ks = jax.random.split(key, 32)
    f32 = jnp.float32
    L = DEPTH

    def nrm(k, shape, scale):
        return jax.random.normal(k, shape, f32) * scale

    def gain(k, shape):
        return 1.0 + nrm(k, shape, 0.02)

    def dt_bias(k):
        dt = jnp.exp(jax.random.uniform(k, (L, SSD_HEADS), f32, math.log(1e-3), math.log(1e-1)))
        return dt + jnp.log(-jnp.expm1(-dt))

    def a_log(k):
        return jnp.log(jax.random.uniform(k, (L, SSD_HEADS), f32, 1.0, 16.0))

    return {
        "x_prompt": jax.random.normal(ks[0], (BATCH, SEQ, D_MODEL), f32),
        "x_sample": jax.random.normal(ks[1], (DEC_BATCH, DEC_SEQ, D_MODEL), f32),
        "norm1": gain(ks[2], (L, D_MODEL)),
        "w_in": nrm(ks[3], (L, D_MODEL, IN_WIDTH), D_MODEL ** -0.5),
        "gla_wg_f": nrm(ks[4], (L, GLA_RANK, GLA_QK), GLA_RANK ** -0.5),
        "gla_bg_f": nrm(ks[5], (L, GLA_QK), 0.1),
        "gla_wg_b": nrm(ks[6], (L, GLA_RANK, GLA_QK), GLA_RANK ** -0.5),
        "gla_bg_b": nrm(ks[7], (L, GLA_QK), 0.1),
        "gla_norm": gain(ks[8], (L, GLA_DV)),
        "diff_qnorm": gain(ks[9], (L, DIFF_DH)),
        "diff_knorm": gain(ks[10], (L, DIFF_DH)),
        "diff_lq1": nrm(ks[11], (L, DIFF_DH), 0.1),
        "diff_lk1": nrm(ks[12], (L, DIFF_DH), 0.1),
        "diff_lq2": nrm(ks[13], (L, DIFF_DH), 0.1),
        "diff_lk2": nrm(ks[14], (L, DIFF_DH), 0.1),
        "diff_subln": gain(ks[15], (L, DIFF_DV)),
        "ssd_conv_w": nrm(ks[16], (L, SSD_CONV, SSD_XBC), SSD_CONV ** -0.5),
        "ssd_conv_b": nrm(ks[17], (L, SSD_XBC), 0.02),
        "ssd_dt_bias_f": dt_bias(ks[18]),
        "ssd_dt_bias_b": dt_bias(ks[19]),
        "ssd_A_log_f": a_log(ks[20]),
        "ssd_A_log_b": a_log(ks[21]),
        "ssd_D": gain(ks[22], (L, SSD_HEADS)),
        "ssd_norm": gain(ks[23], (L, SSD_INNER)),
        "w_out": nrm(ks[24], (L, MIX_WIDTH, D_MODEL), MIX_WIDTH ** -0.5),
        "norm2": gain(ks[25], (L, D_MODEL)),
        "w_mlp1": nrm(ks[26], (L, D_MODEL, D_FF), D_MODEL ** -0.5),
        "w_mlp2": nrm(ks[27], (L, D_FF, D_MODEL), D_FF ** -0.5),
    }


def reference(x_prompt, x_sample, norm1, w_in, gla_wg_f, gla_bg_f, gla_wg_b, gla_bg_b, gla_norm,
              diff_qnorm, diff_knorm, diff_lq1, diff_lk1, diff_lq2, diff_lk2, diff_subln,
              ssd_conv_w, ssd_conv_b, ssd_dt_bias_f, ssd_dt_bias_b, ssd_A_log_f, ssd_A_log_b,
              ssd_D, ssd_norm, w_out, norm2, w_mlp1, w_mlp2):
    weights = (norm1, w_in, gla_wg_f, gla_bg_f, gla_wg_b, gla_bg_b, gla_norm,
               diff_qnorm, diff_knorm, diff_lq1, diff_lk1, diff_lq2, diff_lk2, diff_subln,
               ssd_conv_w, ssd_conv_b, ssd_dt_bias_f, ssd_dt_bias_b, ssd_A_log_f, ssd_A_log_b,
               ssd_D, ssd_norm, w_out, norm2, w_mlp1, w_mlp2)
    y_prompt = trunk(x_prompt, *weights)
    y_sample = trunk(x_sample, *weights)
    return (y_prompt, y_sample)
```

```python
import functools
import math

import numpy as np
import jax
import jax.numpy as jnp
from jax import lax
from jax.experimental import pallas as pl
from jax.experimental.pallas import tpu as pltpu

F32 = jnp.float32
BF16 = jnp.bfloat16

D_MODEL = 1024
DEPTH = 4
GLA_HEADS = 4
GLA_DK = 64
GLA_DV = 128
GLA_RANK = 16
GLA_GATE_NORM = 16.0
DIFF_HEADS = 4
DIFF_DH = 64
DIFF_DV = 2 * DIFF_DH
ROPE_THETA = 500000.0
ROPE_DIM = DIFF_DH // 4
SSD_HEADS = 8
SSD_P = 64
SSD_GROUPS = 2
SSD_E = SSD_HEADS // SSD_GROUPS
SSD_N = 64
SSD_CONV = 5
CHUNK = 64
D_FF = 4 * D_MODEL
EPS = 1e-6

GLA_QK = GLA_HEADS * GLA_DK
GLA_V = GLA_HEADS * GLA_DV
DIFF_QK = DIFF_HEADS * 2 * DIFF_DH
DIFF_V = DIFF_HEADS * DIFF_DV
SSD_INNER = SSD_HEADS * SSD_P
SSD_BC = SSD_GROUPS * SSD_N
SSD_XBC = SSD_INNER + 2 * SSD_BC
MIX_WIDTH = GLA_V + DIFF_V + SSD_INNER
IN_SIZES = (GLA_QK, GLA_QK, GLA_V, GLA_V, 2 * GLA_RANK,
            DIFF_QK, DIFF_QK, DIFF_V,
            SSD_INNER, SSD_XBC, 2 * SSD_HEADS)
IN_OFFSETS = tuple(int(v) for v in np.cumsum(IN_SIZES)[:-1])

LANES = 128

SSD_GW = SSD_E * SSD_P + 2 * SSD_N
C_SX = 0
C_GQ = C_SX + SSD_GROUPS * SSD_GW
C_GK = C_GQ + GLA_QK
C_GV = C_GK + GLA_QK
C_GG = C_GV + GLA_V
C_DQ = C_GG + GLA_V
C_DK = C_DQ + DIFF_QK
C_DV = C_DK + DIFF_QK
C_GLR = C_DV + DIFF_V
C_SDT = C_GLR + LANES
C_SZ = C_SDT + LANES
P_WIDTH = C_SZ + SSD_INNER

VMEM_LIMIT = 56 * 1024 * 1024


def _cparams(sem):
    return pltpu.CompilerParams(dimension_semantics=sem, vmem_limit_bytes=VMEM_LIMIT)


def _dot(a, b):
    return jnp.dot(a, b, preferred_element_type=F32)


def _dot_nt(a, b):
    return lax.dot_general(a, b, (((1,), (1,)), ((), ())), preferred_element_type=F32)


def _dot_tn(a, b):
    return lax.dot_general(a, b, (((0,), (0,)), ((), ())), preferred_element_type=F32)


def _split(x):
    hi = x.astype(BF16)
    lo = (x - hi.astype(F32)).astype(BF16)
    return hi, lo


def _dot_exact_rhs(a01, x):
    hi, lo = _split(x)
    return _dot(a01, hi) + _dot(a01, lo)


def _dot_exact_lhs(x, b01):
    hi, lo = _split(x)
    return _dot(hi, b01) + _dot(lo, b01)


def _silu(x):
    return x * (1.0 / (1.0 + jnp.exp(-x)))


def _softplus(x):
    return jnp.maximum(x, 0.0) + jnp.log(1.0 + jnp.exp(-jnp.abs(x)))


def _rms(x, g):
    ms = jnp.mean(x * x, axis=-1, keepdims=True)
    return x * lax.rsqrt(ms + EPS) * g


IN_TM = 512
IN_TN = 512


def _inproj_kernel(x_ref, g_ref, w_ref, o_ref):
    h = _rms(x_ref[...], g_ref[...]).astype(BF16)
    for j in range(P_WIDTH // IN_TN):
        o_ref[:, j * IN_TN:(j + 1) * IN_TN] = _dot(h, w_ref[:, j * IN_TN:(j + 1) * IN_TN]).astype(BF16)


def _inproj(x2, g, w):
    t = x2.shape[0]
    tm = min(IN_TM, t)
    return pl.pallas_call(
        _inproj_kernel,
        grid=(t // tm,),
        in_specs=[pl.BlockSpec((tm, D_MODEL), lambda i: (i, 0)),
                  pl.BlockSpec((1, D_MODEL), lambda i: (0, 0)),
                  pl.BlockSpec((D_MODEL, P_WIDTH), lambda i: (0, 0))],
        out_specs=pl.BlockSpec((tm, P_WIDTH), lambda i: (i, 0)),
        out_shape=jax.ShapeDtypeStruct((t, P_WIDTH), BF16),
        compiler_params=_cparams(("parallel",)),
        name="inproj",
    )(x2, g, w)


GLA_HP = 2
GLA_QW = GLA_HP * GLA_DK
GLA_VW = GLA_HP * GLA_DV


def _gla_kernel(q_ref, k_ref, v_ref, gg_ref, glr_ref, wg_ref, bg_ref, gn_ref, o_ref,
                logg_ref, oacc_ref, stf_ref, stb_ref, *, seq):
    nc = seq // CHUNK
    gate_rows = min(512, seq)

    def gate_body(i, carry):
        r = pl.multiple_of(i * gate_rows, gate_rows)
        x = _dot(glr_ref[pl.ds(r, gate_rows), :], wg_ref[...]) + bg_ref[...]
        logg_ref[pl.ds(r, gate_rows), :] = -_softplus(-x) * (1.0 / GLA_GATE_NORM)
        return carry

    lax.fori_loop(0, seq // gate_rows, gate_body, 0)
    stf_ref[...] = jnp.zeros_like(stf_ref)
    stb_ref[...] = jnp.zeros_like(stb_ref)

    row = lax.broadcasted_iota(jnp.int32, (CHUNK, CHUNK), 0)
    col = lax.broadcasted_iota(jnp.int32, (CHUNK, CHUNK), 1)
    tri_f = (col <= row).astype(BF16)
    tri_b = (col >= row).astype(BF16)
    row2 = lax.broadcasted_iota(jnp.int32, (GLA_HP * CHUNK, CHUNK), 0) % CHUNK
    col2 = lax.broadcasted_iota(jnp.int32, (GLA_HP * CHUNK, CHUNK), 1)
    mask_f = col2 <= row2
    mask_b = col2 >= row2
    lane = lax.broadcasted_iota(jnp.int32, (CHUNK, GLA_QW), 1)
    head0 = lane < GLA_DK
    st_r = lax.broadcasted_iota(jnp.int32, (GLA_VW, GLA_QW), 0) // GLA_DV
    st_c = lax.broadcasted_iota(jnp.int32, (GLA_VW, GLA_QW), 1) // GLA_DK
    same_head = st_r == st_c
    scale = GLA_DK ** -0.5

    def chunk(r, lg, tri, mask, end_row, st_ref):
        q = q_ref[pl.ds(r, CHUNK), :].astype(F32)
        k = k_ref[pl.ds(r, CHUNK), :].astype(F32)
        v = v_ref[pl.ds(r, CHUNK), :]
        b = _dot_exact_rhs(tri, lg)
        b_end = b[end_row:end_row + 1, :]
        qe = q * jnp.exp(b) * scale
        ke = (k * jnp.exp(-b)).astype(BF16)
        kl = (k * jnp.exp(b_end - b)).astype(BF16)
        q2 = jnp.concatenate([jnp.where(head0, qe, 0.0), jnp.where(head0, 0.0, qe)], axis=0).astype(BF16)
        a = jnp.where(mask, _dot_nt(q2, ke), 0.0).astype(BF16)
        o_intra = jnp.concatenate(
            [_dot(a[h * CHUNK:(h + 1) * CHUNK], v[:, h * GLA_DV:(h + 1) * GLA_DV]) for h in range(GLA_HP)],
            axis=1)
        st = st_ref[...]
        o_state = _dot_nt(qe.astype(BF16), st.astype(BF16))
        upd = _dot_tn(v, kl)
        st_ref[...] = st * jnp.exp(b_end) + jnp.where(same_head, upd, 0.0)
        return o_intra + o_state

    def finalize(r, o):
        gg = gg_ref[pl.ds(r, CHUNK), :].astype(F32)
        parts = [_rms(o[:, h * GLA_DV:(h + 1) * GLA_DV], gn_ref[...]) for h in range(GLA_HP)]
        y = jnp.concatenate(parts, axis=1) * _silu(gg)
        o_ref[pl.ds(r, CHUNK), :] = y.astype(o_ref.dtype)

    def both(c):
        rf = pl.multiple_of(c * CHUNK, CHUNK)
        rb = pl.multiple_of((nc - 1 - c) * CHUNK, CHUNK)
        of = chunk(rf, logg_ref[pl.ds(rf, CHUNK), 0:GLA_QW], tri_f, mask_f, CHUNK - 1, stf_ref)
        ob = chunk(rb, logg_ref[pl.ds(rb, CHUNK), GLA_QW:2 * GLA_QW], tri_b, mask_b, 0, stb_ref)
        return rf, rb, of, ob

    def first_half(c, carry):
        rf, rb, of, ob = both(c)
        oacc_ref[pl.ds(rf, CHUNK), :] = of
        oacc_ref[pl.ds(rb, CHUNK), :] = ob
        return carry

    def second_half(c, carry):
        rf, rb, of, ob = both(c)
        finalize(rf, of + oacc_ref[pl.ds(rf, CHUNK), :])
        finalize(rb, ob + oacc_ref[pl.ds(rb, CHUNK), :])
        return carry

    lax.fori_loop(0, nc // 2, first_half, 0)
    lax.fori_loop(nc // 2, nc, second_half, 0)


def _gla(p3, wg, bg, gn):
    bsz, seq, _ = p3.shape
    npair = GLA_HEADS // GLA_HP
    return pl.pallas_call(
        functools.partial(_gla_kernel, seq=seq),
        grid=(bsz, npair),
        in_specs=[pl.BlockSpec((None, seq, GLA_QW), lambda b, h: (b, 0, C_GQ // GLA_QW + h)),
                  pl.BlockSpec((None, seq, GLA_QW), lambda b, h: (b, 0, C_GK // GLA_QW + h)),
                  pl.BlockSpec((None, seq, GLA_VW), lambda b, h: (b, 0, C_GV // GLA_VW + h)),
                  pl.BlockSpec((None, seq, GLA_VW), lambda b, h: (b, 0, C_GG // GLA_VW + h)),
                  pl.BlockSpec((None, seq, LANES), lambda b, h: (b, 0, C_GLR // LANES)),
                  pl.BlockSpec((None, LANES, 2 * GLA_QW), lambda b, h: (h, 0, 0)),
                  pl.BlockSpec((None, 1, 2 * GLA_QW), lambda b, h: (h, 0, 0)),
                  pl.BlockSpec((1, GLA_DV), lambda b, h: (0, 0))],
        out_specs=pl.BlockSpec((None, seq, GLA_VW), lambda b, h: (b, 0, h)),
        out_shape=jax.ShapeDtypeStruct((bsz, seq, GLA_V), BF16),
        scratch_shapes=[pltpu.VMEM((seq, 2 * GLA_QW), F32),
                        pltpu.VMEM((seq, GLA_VW), F32),
                        pltpu.VMEM((GLA_VW, GLA_QW), F32),
                        pltpu.VMEM((GLA_VW, GLA_QW), F32)],
        compiler_params=_cparams(("parallel", "parallel")),
        name="gla",
    )(p3, p3, p3, p3, p3, wg, bg, gn)


DIFF_QB = 128
DIFF_PREP_ROWS = 256


def _diff_kernel(q_ref, k_ref, v_ref, cos_ref, sin_ref, qn_ref, kn_ref, lqk_ref, sub_ref, o_ref,
                 qp_ref, kp_ref, *, seq, lambda_init):
    qi = pl.program_id(2)
    lane = lax.broadcasted_iota(jnp.int32, (1, DIFF_DV), 1)
    comp0 = lane < DIFF_DH
    in_grp = lane % DIFF_DH
    first_half = in_grp < ROPE_DIM // 2

    def prep(x, gain, cos, sin):
        x2 = x * x
        s_all = jnp.sum(x2, axis=-1, keepdims=True)
        s_lo = jnp.sum(jnp.where(comp0, x2, 0.0), axis=-1, keepdims=True)
        ms = jnp.where(comp0, s_lo, s_all - s_lo) * (1.0 / DIFF_DH)
        xn = x * lax.rsqrt(ms + EPS) * gain
        partner = jnp.where(first_half,
                            pltpu.roll(xn, DIFF_DV - ROPE_DIM // 2, axis=1),
                            pltpu.roll(xn, ROPE_DIM // 2, axis=1))
        return xn * cos + partner * sin

    @pl.when(qi == 0)
    def _():
        rows = min(DIFF_PREP_ROWS, seq)

        def body(i, carry):
            r = pl.multiple_of(i * rows, rows)
            cos = cos_ref[pl.ds(r, rows), :]
            sin = sin_ref[pl.ds(r, rows), :]
            qp = prep(q_ref[pl.ds(r, rows), :].astype(F32), qn_ref[...], cos, sin) * (DIFF_DH ** -0.5)
            kp = prep(k_ref[pl.ds(r, rows), :].astype(F32), kn_ref[...], cos, sin)
            qp_ref[pl.ds(r, rows), :] = qp.astype(BF16)
            kp_ref[pl.ds(r, rows), :] = kp.astype(BF16)
            return carry

        lax.fori_loop(0, seq // rows, body, 0)

    lqk = lqk_ref[...]
    l1 = jnp.sum(lqk[0:1] * lqk[1:2], axis=-1, keepdims=True)
    l2 = jnp.sum(lqk[2:3] * lqk[3:4], axis=-1, keepdims=True)
    lam = jnp.exp(l1) - jnp.exp(l2) + lambda_init

    r0 = pl.multiple_of(qi * DIFF_QB, DIFF_QB)
    qb = qp_ref[pl.ds(r0, DIFF_QB), :]
    zero = jnp.zeros_like(qb)
    qz = jnp.concatenate([jnp.where(comp0, qb, zero), jnp.where(comp0, zero, qb)], axis=0)
    s = _dot_nt(qz, kp_ref[...])
    m = jnp.max(s, axis=-1, keepdims=True)
    e = jnp.exp(s - m)
    l = jnp.sum(e, axis=-1, keepdims=True)
    rinv = 1.0 / l
    w = e[0:DIFF_QB] * rinv[0:DIFF_QB] - e[DIFF_QB:] * (lam * rinv[DIFF_QB:])
    o = _dot(w.astype(BF16), v_ref[...])
    o = _rms(o, sub_ref[...]) * (1.0 - lambda_init)
    o_ref[...] = o.astype(o_ref.dtype)


def _diff(p3, cos_t, sin_t, qn, kn, lqk, sub, lambda_init):
    bsz, seq, _ = p3.shape
    qb = DIFF_QB
    return pl.pallas_call(
        functools.partial(_diff_kernel, seq=seq, lambda_init=lambda_init),
        grid=(bsz, DIFF_HEADS, seq // qb),
        in_specs=[pl.BlockSpec((None, seq, DIFF_DV), lambda b, h, i: (b, 0, C_DQ // DIFF_DV + h)),
                  pl.BlockSpec((None, seq, DIFF_DV), lambda b, h, i: (b, 0, C_DK // DIFF_DV + h)),
                  pl.BlockSpec((None, seq, DIFF_DV), lambda b, h, i: (b, 0, C_DV // DIFF_DV + h)),
                  pl.BlockSpec((seq, DIFF_DV), lambda b, h, i: (0, 0)),
                  pl.BlockSpec((seq, DIFF_DV), lambda b, h, i: (0, 0)),
                  pl.BlockSpec((1, DIFF_DV), lambda b, h, i: (0, 0)),
                  pl.BlockSpec((1, DIFF_DV), lambda b, h, i: (0, 0)),
                  pl.BlockSpec((4, DIFF_DH), lambda b, h, i: (0, 0)),
                  pl.BlockSpec((1, DIFF_DV), lambda b, h, i: (0, 0))],
        out_specs=pl.BlockSpec((None, qb, DIFF_DV), lambda b, h, i: (b, i, h)),
        out_shape=jax.ShapeDtypeStruct((bsz, seq, DIFF_V), BF16),
        scratch_shapes=[pltpu.VMEM((seq, DIFF_DV), BF16),
                        pltpu.VMEM((seq, DIFF_DV), BF16)],
        compiler_params=_cparams(("parallel", "parallel", "arbitrary")),
        name="diffattn",
    )(p3, p3, p3, cos_t, sin_t, qn, kn, lqk, sub)


SSD_XW = SSD_E * SSD_P
SSD_HALO = 8
SSD_CONV_ROWS = 256


def _ssd_kernel(xbc_ref, sdt_ref, cw_ref, cb_ref, dtb_ref, ef_ref, eb_ref, af_ref, ab_ref, dsk_ref, o_ref,
                xpad_ref, xc_ref, dt_ref, yacc_ref, stf_ref, stb_ref, *, seq):
    nc = seq // CHUNK
    rows = min(SSD_CONV_ROWS, seq)
    pad = SSD_CONV // 2

    xpad_ref[0:SSD_HALO, :] = jnp.zeros((SSD_HALO, SSD_GW), F32)
    xpad_ref[seq + SSD_HALO:seq + 2 * SSD_HALO, :] = jnp.zeros((SSD_HALO, SSD_GW), F32)

    def copy_body(i, carry):
        r = pl.multiple_of(i * rows, rows)
        xpad_ref[pl.ds(r + SSD_HALO, rows), :] = xbc_ref[pl.ds(r, rows), :].astype(F32)
        dt_ref[pl.ds(r, rows), :] = _softplus(sdt_ref[pl.ds(r, rows), :].astype(F32) + dtb_ref[...])
        return carry

    lax.fori_loop(0, seq // rows, copy_body, 0)

    def conv_body(i, carry):
        r = pl.multiple_of(i * rows, rows)
        win = xpad_ref[pl.ds(r, rows + 2 * SSD_HALO), :]
        acc = cb_ref[...] + win[SSD_HALO - pad:SSD_HALO - pad + rows] * cw_ref[0:1, :]
        for w in range(1, SSD_CONV):
            acc = acc + win[SSD_HALO - pad + w:SSD_HALO - pad + w + rows] * cw_ref[w:w + 1, :]
        xc_ref[pl.ds(r, rows), :] = _silu(acc)
        return carry

    lax.fori_loop(0, seq // rows, conv_body, 0)
    stf_ref[...] = jnp.zeros_like(stf_ref)
    stb_ref[...] = jnp.zeros_like(stb_ref)

    row = lax.broadcasted_iota(jnp.int32, (CHUNK, CHUNK), 0)
    col = lax.broadcasted_iota(jnp.int32, (CHUNK, CHUNK), 1)
    tri_f = (col <= row).astype(BF16)
    tri_b = (col >= row).astype(BF16)
    ones = jnp.ones((CHUNK, CHUNK), BF16)
    row4 = lax.broadcasted_iota(jnp.int32, (CHUNK, SSD_XW), 0)
    col4 = lax.broadcasted_iota(jnp.int32, (CHUNK, SSD_XW), 1) % CHUNK
    mask_f = col4 <= row4
    mask_b = col4 >= row4
    bd_r = lax.broadcasted_iota(jnp.int32, (SSD_E * CHUNK, SSD_XW), 0) // CHUNK
    bd_c = lax.broadcasted_iota(jnp.int32, (SSD_E * CHUNK, SSD_XW), 1) // SSD_P
    same_head = bd_r == bd_c

    def chunk(r, e_ref, a_ref, tri, mask_ij, mask_rowsum, end_row, st_ref, with_skip):
        xs = xc_ref[pl.ds(r, CHUNK), 0:SSD_XW]
        bm = xc_ref[pl.ds(r, CHUNK), SSD_XW:SSD_XW + SSD_N].astype(BF16)
        cm = xc_ref[pl.ds(r, CHUNK), SSD_XW + SSD_N:SSD_XW + 2 * SSD_N].astype(BF16)
        dt4 = _dot_exact_lhs(dt_ref[pl.ds(r, CHUNK), :], e_ref[...])
        la4 = dt4 * a_ref[...]
        acol = _dot_exact_rhs(tri, la4)
        arow = _dot_exact_rhs(ones, jnp.where(mask_rowsum, la4, 0.0))
        lm = jnp.exp(jnp.where(mask_ij, acol - arow, -jnp.inf))
        cb4 = _dot_nt(cm, jnp.concatenate([bm] * SSD_E, axis=0))
        m4 = (cb4 * lm).astype(BF16)
        xd = xs * dt4
        xbd = jnp.where(same_head, jnp.concatenate([xd] * SSD_E, axis=0), 0.0).astype(BF16)
        y_intra = _dot(m4, xbd)
        st = st_ref[...]
        y_state = _dot(cm, st.astype(BF16)) * jnp.exp(acol)
        a_end = acol[end_row:end_row + 1, :]
        wx = (xd * jnp.exp(a_end - acol)).astype(BF16)
        st_ref[...] = st * jnp.exp(a_end) + _dot_tn(bm, wx)
        y = y_intra + y_state
        return y + dsk_ref[...] * xs if with_skip else y

    def both(c):
        rf = pl.multiple_of(c * CHUNK, CHUNK)
        rb = pl.multiple_of((nc - 1 - c) * CHUNK, CHUNK)
        yf = chunk(rf, ef_ref, af_ref, tri_f, mask_f, mask_b, CHUNK - 1, stf_ref, True)
        yb = chunk(rb, eb_ref, ab_ref, tri_b, mask_b, mask_f, 0, stb_ref, False)
        return rf, rb, yf, yb

    def first_half(c, carry):
        rf, rb, yf, yb = both(c)
        yacc_ref[pl.ds(rf, CHUNK), :] = yf
        yacc_ref[pl.ds(rb, CHUNK), :] = yb
        return carry

    def second_half(c, carry):
        rf, rb, yf, yb = both(c)
        o_ref[pl.ds(rf, CHUNK), :] = (yf + yacc_ref[pl.ds(rf, CHUNK), :]).astype(o_ref.dtype)
        o_ref[pl.ds(rb, CHUNK), :] = (yb + yacc_ref[pl.ds(rb, CHUNK), :]).astype(o_ref.dtype)
        return carry

    lax.fori_loop(0, nc // 2, first_half, 0)
    lax.fori_loop(nc // 2, nc, second_half, 0)


def _ssd(p3, cw, cb, dtb, ef, eb, af, ab, dsk):
    bsz, seq, _ = p3.shape
    g3 = lambda shape: pl.BlockSpec((None,) + shape, lambda b, g: (g, 0, 0))
    return pl.pallas_call(
        functools.partial(_ssd_kernel, seq=seq),
        grid=(bsz, SSD_GROUPS),
        in_specs=[pl.BlockSpec((None, seq, SSD_GW), lambda b, g: (b, 0, C_SX // SSD_GW + g)),
                  pl.BlockSpec((None, seq, LANES), lambda b, g: (b, 0, C_SDT // LANES)),
                  g3((SSD_CONV, SSD_GW)), g3((1, SSD_GW)),
                  pl.BlockSpec((1, LANES), lambda b, g: (0, 0)),
                  g3((LANES, SSD_XW)), g3((LANES, SSD_XW)),
                  g3((1, SSD_XW)), g3((1, SSD_XW)), g3((1, SSD_XW))],
        out_specs=pl.BlockSpec((None, seq, SSD_XW), lambda b, g: (b, 0, g)),
        out_shape=jax.ShapeDtypeStruct((bsz, seq, SSD_INNER), BF16),
        scratch_shapes=[pltpu.VMEM((seq + 2 * SSD_HALO, SSD_GW), F32),
                        pltpu.VMEM((seq, SSD_GW), F32),
                        pltpu.VMEM((seq, LANES), F32),
                        pltpu.VMEM((seq, SSD_XW), F32),
                        pltpu.VMEM((SSD_N, SSD_XW), F32),
                        pltpu.VMEM((SSD_N, SSD_XW), F32)],
        compiler_params=_cparams(("parallel", "parallel")),
        name="ssd",
    )(p3, p3, cw, cb, dtb, ef, eb, af, ab, dsk)


MLP_TM = 512
MLP_TF = 1024


def _outmlp_kernel(x_ref, gla_ref, diff_ref, ssd_ref, z_ref, wo_ref, sn_ref, g2_ref, w1_ref, w2_ref, o_ref,
                   h_ref):
    k = pl.program_id(1)

    @pl.when(k == 0)
    def _():
        t = ssd_ref[...].astype(F32) * _silu(z_ref[...].astype(F32))
        sn = _rms(t, sn_ref[...]).astype(BF16)
        xn = (x_ref[...]
              + _dot(gla_ref[...], wo_ref[0:GLA_V, :])
              + _dot(diff_ref[...], wo_ref[GLA_V:GLA_V + DIFF_V, :])
              + _dot(sn, wo_ref[GLA_V + DIFF_V:MIX_WIDTH, :]))
        h_ref[...] = _rms(xn, g2_ref[...]).astype(BF16)
        o_ref[...] = xn

    a = _dot(h_ref[...], w1_ref[...])
    a = jnp.square(jnp.maximum(a, 0.0)).astype(BF16)
    o_ref[...] += _dot(a, w2_ref[...])


def _outmlp(x2, gla2, diff2, ssd2, p2, wo, sn, g2, w1, w2):
    t = x2.shape[0]
    tm = min(MLP_TM, t)
    row = lambda w: pl.BlockSpec((tm, w), lambda i, k: (i, 0))
    return pl.pallas_call(
        _outmlp_kernel,
        grid=(t // tm, D_FF // MLP_TF),
        in_specs=[row(D_MODEL), row(GLA_V), row(DIFF_V), row(SSD_INNER),
                  pl.BlockSpec((tm, SSD_INNER), lambda i, k: (i, C_SZ // SSD_INNER)),
                  pl.BlockSpec((MIX_WIDTH, D_MODEL), lambda i, k: (0, 0)),
                  pl.BlockSpec((1, SSD_INNER), lambda i, k: (0, 0)),
                  pl.BlockSpec((1, D_MODEL), lambda i, k: (0, 0)),
                  pl.BlockSpec((D_MODEL, MLP_TF), lambda i, k: (0, k)),
                  pl.BlockSpec((MLP_TF, D_MODEL), lambda i, k: (k, 0))],
        out_specs=pl.BlockSpec((tm, D_MODEL), lambda i, k: (i, 0)),
        out_shape=jax.ShapeDtypeStruct((t, D_MODEL), F32),
        scratch_shapes=[pltpu.VMEM((tm, D_MODEL), BF16)],
        compiler_params=_cparams(("parallel", "arbitrary")),
        name="outmlp",
    )(x2, gla2, diff2, ssd2, p2, wo, sn, g2, w1, w2)


def _prep_params(norm1, w_in, gla_wg_f, gla_bg_f, gla_wg_b, gla_bg_b, gla_norm,
                 diff_qnorm, diff_knorm, diff_lq1, diff_lk1, diff_lq2, diff_lk2, diff_subln,
                 ssd_conv_w, ssd_conv_b, ssd_dt_bias_f, ssd_dt_bias_b, ssd_A_log_f, ssd_A_log_b,
                 ssd_D, ssd_norm, w_out, norm2, w_mlp1, w_mlp2):
    depth = w_in.shape[0]
    (gq, gk, gv, gg, glr, dq, dk, dv, sz, sxbc, sdt) = jnp.split(w_in, IN_OFFSETS, axis=-1)

    def group_cols(a):
        xs, bm, cm = a[..., :SSD_INNER], a[..., SSD_INNER:SSD_INNER + SSD_BC], a[..., SSD_INNER + SSD_BC:]
        parts = []
        for g in range(SSD_GROUPS):
            parts += [xs[..., g * SSD_XW:(g + 1) * SSD_XW], bm[..., g * SSD_N:(g + 1) * SSD_N],
                      cm[..., g * SSD_N:(g + 1) * SSD_N]]
        return parts

    def pad_cols(a, width):
        return jnp.pad(a, [(0, 0)] * (a.ndim - 1) + [(0, width - a.shape[-1])])

    w_p = jnp.concatenate(group_cols(sxbc) + [gq, gk, gv, gg, dq, dk, dv,
                                              pad_cols(glr, LANES), pad_cols(sdt, LANES), sz],
                          axis=-1).astype(BF16)

    npair = GLA_HEADS // GLA_HP
    wg = jnp.zeros((depth, npair, LANES, 2 * GLA_QW), F32)
    bgs = []
    for hp in range(npair):
        cs = slice(hp * GLA_QW, (hp + 1) * GLA_QW)
        wg = wg.at[:, hp, 0:GLA_RANK, 0:GLA_QW].set(gla_wg_f[:, :, cs])
        wg = wg.at[:, hp, GLA_RANK:2 * GLA_RANK, GLA_QW:].set(gla_wg_b[:, :, cs])
        bgs.append(jnp.concatenate([gla_bg_f[:, cs], gla_bg_b[:, cs]], axis=-1))
    wg = wg.astype(BF16)
    bg = jnp.stack(bgs, axis=1)[:, :, None, :]
    gn = gla_norm[:, None, :]

    qn = jnp.tile(diff_qnorm, (1, 2))[:, None, :]
    kn = jnp.tile(diff_knorm, (1, 2))[:, None, :]
    lqk = jnp.stack([diff_lq1, diff_lk1, diff_lq2, diff_lk2], axis=1)
    sub = diff_subln[:, None, :]

    cw = jnp.concatenate(group_cols(ssd_conv_w), axis=-1).reshape(depth, SSD_CONV, SSD_GROUPS, SSD_GW)
    cw = cw.transpose(0, 2, 1, 3)
    cb = jnp.concatenate(group_cols(ssd_conv_b), axis=-1).reshape(depth, SSD_GROUPS, 1, SSD_GW)
    dtb = pad_cols(jnp.concatenate([ssd_dt_bias_f, ssd_dt_bias_b], axis=-1), LANES)[:, None, :]

    e_np = np.zeros((2, SSD_GROUPS, LANES, SSD_XW), np.float32)
    for d in range(2):
        for g in range(SSD_GROUPS):
            for e in range(SSD_E):
                e_np[d, g, d * SSD_HEADS + g * SSD_E + e, e * SSD_P:(e + 1) * SSD_P] = 1.0
    e_f = jnp.asarray(e_np[0], BF16)
    e_b = jnp.asarray(e_np[1], BF16)

    def per_lane(a):
        return jnp.repeat(a.reshape(depth, SSD_GROUPS, SSD_E), SSD_P, axis=-1)[:, :, None, :]

    a_f = per_lane(-jnp.exp(ssd_A_log_f))
    a_b = per_lane(-jnp.exp(ssd_A_log_b))
    dsk = per_lane(ssd_D)

    return dict(norm1=norm1[:, None, :], w_p=w_p, wg=wg, bg=bg, gn=gn, qn=qn, kn=kn, lqk=lqk, sub=sub,
                cw=cw, cb=cb, dtb=dtb, e_f=e_f, e_b=e_b, a_f=a_f, a_b=a_b, dsk=dsk,
                sn=ssd_norm[:, None, :], wo=w_out.astype(BF16), norm2=norm2[:, None, :],
                w1=w_mlp1.astype(BF16), w2=w_mlp2.astype(BF16))


def _rope_tables(seq):
    inv = ROPE_THETA ** (-jnp.arange(0, ROPE_DIM, 2, dtype=F32) / ROPE_DIM)
    ang = jnp.arange(seq, dtype=F32)[:, None] * inv[None, :]
    lane = np.arange(DIFF_DV) % DIFF_DH
    idx = lane % (ROPE_DIM // 2)
    in_rope = lane < ROPE_DIM
    sign = np.where(lane < ROPE_DIM // 2, -1.0, 1.0).astype(np.float32)
    cos_t = jnp.where(in_rope[None, :], jnp.cos(ang)[:, idx], 1.0)
    sin_t = jnp.where(in_rope[None, :], jnp.sin(ang)[:, idx] * sign[None, :], 0.0)
    return cos_t.astype(F32), sin_t.astype(F32)


def _trunk(x, pr):
    bsz, seq, _ = x.shape
    cos_t, sin_t = _rope_tables(seq)
    x2 = x.reshape(bsz * seq, D_MODEL)
    for l in range(DEPTH):
        p2 = _inproj(x2, pr["norm1"][l], pr["w_p"][l])
        p3 = p2.reshape(bsz, seq, P_WIDTH)
        o_gla = _gla(p3, pr["wg"][l], pr["bg"][l], pr["gn"][l])
        lambda_init = 0.8 - 0.6 * math.exp(-0.3 * l)
        o_diff = _diff(p3, cos_t, sin_t, pr["qn"][l], pr["kn"][l], pr["lqk"][l], pr["sub"][l], lambda_init)
        o_ssd = _ssd(p3, pr["cw"][l], pr["cb"][l], pr["dtb"][l], pr["e_f"], pr["e_b"],
                     pr["a_f"][l], pr["a_b"][l], pr["dsk"][l])
        x2 = _outmlp(x2, o_gla.reshape(bsz * seq, GLA_V), o_diff.reshape(bsz * seq, DIFF_V),
                     o_ssd.reshape(bsz * seq, SSD_INNER), p2, pr["wo"][l], pr["sn"][l], pr["norm2"][l],
                     pr["w1"][l], pr["w2"][l])
    return x2.reshape(bsz, seq, D_MODEL)


def kernel(x_prompt, x_sample, norm1, w_in, gla_wg_f, gla_bg_f, gla_wg_b, gla_bg_b, gla_norm, diff_qnorm, diff_knorm, diff_lq1, diff_lk1, diff_lq2, diff_lk2, diff_subln, ssd_conv_w, ssd_conv_b, ssd_dt_bias_f, ssd_dt_bias_b, ssd_A_log_f, ssd_A_log_b, ssd_D, ssd_norm, w_out, norm2, w_mlp1, w_mlp2):
    pr = _prep_params(norm1, w_in, gla_wg_f, gla_bg_f, gla_wg_b, gla_bg_b, gla_norm,
                      diff_qnorm, diff_knorm, diff_lq1, diff_lk1, diff_lq2, diff_lk2, diff_subln,
                      ssd_conv_w, ssd_conv_b, ssd_dt_bias_f, ssd_dt_bias_b, ssd_A_log_f, ssd_A_log_b,
                      ssd_D, ssd_norm, w_out, norm2, w_mlp1, w_mlp2)
    return (_trunk(x_prompt, pr), _trunk(x_sample, pr))
```

```python
import functools
import math

import numpy as np
import jax
import jax.numpy as jnp
from jax import lax
from jax.experimental import pallas as pl
from jax.experimental.pallas import tpu as pltpu

F32 = jnp.float32
BF16 = jnp.bfloat16

D_MODEL = 1024
DEPTH = 4
GLA_HEADS = 4
GLA_DK = 64
GLA_DV = 128
GLA_RANK = 16
GLA_GATE_NORM = 16.0
DIFF_HEADS = 4
DIFF_DH = 64
DIFF_DV = 2 * DIFF_DH
ROPE_THETA = 500000.0
ROPE_DIM = DIFF_DH // 4
SSD_HEADS = 8
SSD_P = 64
SSD_GROUPS = 2
SSD_E = SSD_HEADS // SSD_GROUPS
SSD_N = 64
SSD_CONV = 5
CHUNK = 64
D_FF = 4 * D_MODEL
EPS = 1e-6

GLA_QK = GLA_HEADS * GLA_DK
GLA_V = GLA_HEADS * GLA_DV
DIFF_QK = DIFF_HEADS * 2 * DIFF_DH
DIFF_V = DIFF_HEADS * DIFF_DV
SSD_INNER = SSD_HEADS * SSD_P
SSD_BC = SSD_GROUPS * SSD_N
SSD_XBC = SSD_INNER + 2 * SSD_BC
MIX_WIDTH = GLA_V + DIFF_V + SSD_INNER
IN_SIZES = (GLA_QK, GLA_QK, GLA_V, GLA_V, 2 * GLA_RANK,
            DIFF_QK, DIFF_QK, DIFF_V,
            SSD_INNER, SSD_XBC, 2 * SSD_HEADS)
IN_OFFSETS = tuple(int(v) for v in np.cumsum(IN_SIZES)[:-1])

LANES = 128

SSD_GW = SSD_E * SSD_P + 2 * SSD_N
C_SX = 0
C_GQ = C_SX + SSD_GROUPS * SSD_GW
C_GK = C_GQ + GLA_QK
C_GV = C_GK + GLA_QK
C_GG = C_GV + GLA_V
C_DQ = C_GG + GLA_V
C_DK = C_DQ + DIFF_QK
C_DV = C_DK + DIFF_QK
C_GLR = C_DV + DIFF_V
C_SDT = C_GLR + LANES
C_SZ = C_SDT + LANES
P_WIDTH = C_SZ + SSD_INNER

VMEM_LIMIT = 56 * 1024 * 1024
SCAN_UNROLL = 2
GROUP = 4
GRP_ROWS = GROUP * CHUNK
PAR_GROUPS = 4


def _cparams(sem):
    return pltpu.CompilerParams(dimension_semantics=sem, vmem_limit_bytes=VMEM_LIMIT)


def _dot(a, b):
    return jnp.dot(a, b, preferred_element_type=F32)


def _dot_nt(a, b):
    return lax.dot_general(a, b, (((1,), (1,)), ((), ())), preferred_element_type=F32)


def _dot_tn(a, b):
    return lax.dot_general(a, b, (((0,), (0,)), ((), ())), preferred_element_type=F32)


def _split(x):
    hi = x.astype(BF16)
    lo = (x - hi.astype(F32)).astype(BF16)
    return hi, lo


def _dot_exact_rhs(a01, x):
    hi, lo = _split(x)
    return _dot(a01, hi) + _dot(a01, lo)


def _dot_exact_lhs(x, b01):
    hi, lo = _split(x)
    return _dot(hi, b01) + _dot(lo, b01)


def _silu(x):
    return x * (1.0 / (1.0 + jnp.exp(-x)))


def _softplus(x):
    return jnp.maximum(x, 0.0) + jnp.log(1.0 + jnp.exp(-jnp.abs(x)))


def _rms(x, g):
    ms = jnp.mean(x * x, axis=-1, keepdims=True)
    return x * lax.rsqrt(ms + EPS) * g


IN_TM = 512
IN_TN = 512


def _inproj_kernel(x_ref, g_ref, w_ref, o_ref):
    h = _rms(x_ref[...], g_ref[...]).astype(BF16)
    for j in range(P_WIDTH // IN_TN):
        o_ref[:, j * IN_TN:(j + 1) * IN_TN] = _dot(h, w_ref[:, j * IN_TN:(j + 1) * IN_TN]).astype(BF16)


def _inproj(x2, g, w):
    t = x2.shape[0]
    tm = min(IN_TM, t)
    return pl.pallas_call(
        _inproj_kernel,
        grid=(t // tm,),
        in_specs=[pl.BlockSpec((tm, D_MODEL), lambda i: (i, 0)),
                  pl.BlockSpec((1, D_MODEL), lambda i: (0, 0)),
                  pl.BlockSpec((D_MODEL, P_WIDTH), lambda i: (0, 0))],
        out_specs=pl.BlockSpec((tm, P_WIDTH), lambda i: (i, 0)),
        out_shape=jax.ShapeDtypeStruct((t, P_WIDTH), BF16),
        compiler_params=_cparams(("parallel",)),
        name="inproj",
    )(x2, g, w)


GLA_HP = 2
GLA_QW = GLA_HP * GLA_DK
GLA_VW = GLA_HP * GLA_DV


def _gla_kernel(q_ref, k_ref, v_ref, gg_ref, glr_ref, wg_ref, bg_ref, gn_ref, o_ref,
                upd_ref, qe_ref, oacc_ref, d_ref, st_ref, *, seq):
    nc = seq // CHUNK
    ng = seq // GRP_ROWS
    scale = GLA_DK ** -0.5

    row = lax.broadcasted_iota(jnp.int32, (GRP_ROWS, GRP_ROWS), 0)
    col = lax.broadcasted_iota(jnp.int32, (GRP_ROWS, GRP_ROWS), 1)
    tri = ((row // CHUNK == col // CHUNK) & (col <= row)).astype(BF16)
    row2 = lax.broadcasted_iota(jnp.int32, (GLA_HP * GRP_ROWS, GRP_ROWS), 0) % GRP_ROWS
    col2 = lax.broadcasted_iota(jnp.int32, (GLA_HP * GRP_ROWS, GRP_ROWS), 1)
    same_chunk = row2 // CHUNK == col2 // CHUNK
    masks = (same_chunk & (col2 <= row2), same_chunk & (col2 >= row2))
    head0 = lax.broadcasted_iota(jnp.int32, (GRP_ROWS, GLA_QW), 1) < GLA_DK
    is_fwd = lax.broadcasted_iota(jnp.int32, (GRP_ROWS, 2 * GLA_QW), 1) < GLA_QW
    st_r = lax.broadcasted_iota(jnp.int32, (GLA_VW, GLA_QW), 0) // GLA_DV
    st_c = lax.broadcasted_iota(jnp.int32, (GLA_VW, GLA_QW), 1) // GLA_DK
    same_head = st_r == st_c

    n_par = PAR_GROUPS if ng % PAR_GROUPS == 0 else 1
    dirs = (slice(0, GLA_QW), slice(GLA_QW, 2 * GLA_QW))

    def gates(r):
        x = _dot(glr_ref[pl.ds(r, GRP_ROWS), :], wg_ref[...]) + bg_ref[...]
        return -_softplus(-x) * (1.0 / GLA_GATE_NORM)

    def decayed(r, lg, pre):
        tot = jnp.concatenate(
            [jnp.broadcast_to(pre[g * CHUNK + CHUNK - 1:(g + 1) * CHUNK, :], (CHUNK, 2 * GLA_QW))
             for g in range(GROUP)], axis=0)
        b = jnp.where(is_fwd, pre, tot - pre + lg)
        q = q_ref[pl.ds(r, GRP_ROWS), :].astype(F32)
        k = k_ref[pl.ds(r, GRP_ROWS), :].astype(F32)
        qe = jnp.concatenate([q, q], axis=1) * jnp.exp(b) * scale
        kk = jnp.concatenate([k, k], axis=1)
        ke = (kk * jnp.exp(-b)).astype(BF16)
        kl = (kk * jnp.exp(tot - b)).astype(BF16)
        q2 = [jnp.concatenate([jnp.where(head0, qe[:, sl], 0.0), jnp.where(head0, 0.0, qe[:, sl])],
                              axis=0).astype(BF16) for sl in dirs]
        return tot, qe.astype(BF16), ke, kl, q2

    def phase_a(i, carry):
        ts = [i + j * (ng // n_par) for j in range(n_par)]
        rs = [pl.multiple_of(t * GRP_ROWS, GRP_ROWS) for t in ts]
        lgs = [gates(r) for r in rs]
        pres = [_dot_exact_rhs(tri, lg) for lg in lgs]
        dec = [decayed(r, lg, pre) for r, lg, pre in zip(rs, lgs, pres)]
        vs = [v_ref[pl.ds(r, GRP_ROWS), :] for r in rs]
        scores = [[_dot_nt(q2[d], ke[:, dirs[d]]) for d in range(2)] for (_, _, ke, _, q2) in dec]
        upds = [[[_dot_tn(v[g * CHUNK:(g + 1) * CHUNK], kl[g * CHUNK:(g + 1) * CHUNK, dirs[d]])
                  for g in range(GROUP)] for d in range(2)] for v, (_, _, _, kl, _) in zip(vs, dec)]
        for t, r, v, (tot, qe, _, _, _), sc, up in zip(ts, rs, vs, dec, scores, upds):
            qe_ref[pl.ds(r, GRP_ROWS), :] = qe
            o_sum = None
            for d in range(2):
                a = jnp.where(masks[d], sc[d], 0.0).astype(BF16)
                o_d = jnp.concatenate(
                    [_dot(a[h * GRP_ROWS:(h + 1) * GRP_ROWS], v[:, h * GLA_DV:(h + 1) * GLA_DV])
                     for h in range(GLA_HP)], axis=1)
                o_sum = o_d if o_sum is None else o_sum + o_d
                for g in range(GROUP):
                    upd_ref[d * nc + t * GROUP + g] = jnp.where(same_head, up[d][g], 0.0)
            oacc_ref[pl.ds(r, GRP_ROWS), :] = o_sum
            for g in range(GROUP):
                d_ref[t * GROUP + g] = jnp.exp(tot[g * CHUNK:g * CHUNK + 8, :])
        return carry

    lax.fori_loop(0, ng // n_par, phase_a, 0)

    st_ref[...] = jnp.zeros_like(st_ref)

    def phase_b(c, carry):
        cb = nc - 1 - c
        for d, idx, cd in ((0, c, c), (1, nc + cb, cb)):
            dec = d_ref[cd][0:1, d * GLA_QW:(d + 1) * GLA_QW]
            u = upd_ref[idx]
            s = st_ref[d]
            upd_ref[idx] = s
            st_ref[d] = s * dec + u
        return carry

    lax.fori_loop(0, nc, phase_b, 0, unroll=SCAN_UNROLL)

    def phase_c(i, carry):
        ts = [i + j * (ng // n_par) for j in range(n_par)]
        rs = [pl.multiple_of(t * GRP_ROWS, GRP_ROWS) for t in ts]
        inter = []
        for t, r in zip(ts, rs):
            parts = []
            for g in range(GROUP):
                qe = qe_ref[pl.ds(r + g * CHUNK, CHUNK), :]
                ods = [_dot_nt(qe[:, dirs[d]], upd_ref[d * nc + t * GROUP + g].astype(BF16)) for d in range(2)]
                parts.append(ods[0] + ods[1])
            inter.append(parts)
        for r, parts in zip(rs, inter):
            o = oacc_ref[pl.ds(r, GRP_ROWS), :] + jnp.concatenate(parts, axis=0)
            gg = gg_ref[pl.ds(r, GRP_ROWS), :].astype(F32)
            normed = [_rms(o[:, h * GLA_DV:(h + 1) * GLA_DV], gn_ref[...]) for h in range(GLA_HP)]
            y = jnp.concatenate(normed, axis=1) * _silu(gg)
            o_ref[pl.ds(r, GRP_ROWS), :] = y.astype(o_ref.dtype)
        return carry

    lax.fori_loop(0, ng // n_par, phase_c, 0)


def _gla(p3, wg, bg, gn):
    bsz, seq, _ = p3.shape
    assert seq % GRP_ROWS == 0
    nc = seq // CHUNK
    npair = GLA_HEADS // GLA_HP
    return pl.pallas_call(
        functools.partial(_gla_kernel, seq=seq),
        grid=(bsz, npair),
        in_specs=[pl.BlockSpec((None, seq, GLA_QW), lambda b, h: (b, 0, C_GQ // GLA_QW + h)),
                  pl.BlockSpec((None, seq, GLA_QW), lambda b, h: (b, 0, C_GK // GLA_QW + h)),
                  pl.BlockSpec((None, seq, GLA_VW), lambda b, h: (b, 0, C_GV // GLA_VW + h)),
                  pl.BlockSpec((None, seq, GLA_VW), lambda b, h: (b, 0, C_GG // GLA_VW + h)),
                  pl.BlockSpec((None, seq, LANES), lambda b, h: (b, 0, C_GLR // LANES)),
                  pl.BlockSpec((None, LANES, 2 * GLA_QW), lambda b, h: (h, 0, 0)),
                  pl.BlockSpec((None, 1, 2 * GLA_QW), lambda b, h: (h, 0, 0)),
                  pl.BlockSpec((1, GLA_DV), lambda b, h: (0, 0))],
        out_specs=pl.BlockSpec((None, seq, GLA_VW), lambda b, h: (b, 0, h)),
        out_shape=jax.ShapeDtypeStruct((bsz, seq, GLA_V), BF16),
        scratch_shapes=[pltpu.VMEM((2 * nc, GLA_VW, GLA_QW), F32),
                        pltpu.VMEM((seq, 2 * GLA_QW), BF16),
                        pltpu.VMEM((seq, GLA_VW), F32),
                        pltpu.VMEM((nc, 8, 2 * GLA_QW), F32),
                        pltpu.VMEM((2, GLA_VW, GLA_QW), F32)],
        compiler_params=_cparams(("parallel", "parallel")),
        name="gla",
    )(p3, p3, p3, p3, p3, wg, bg, gn)


DIFF_QB = 512
DIFF_NSUB = 4
DIFF_PREP_ROWS = 256
LOG2E = math.log2(math.e)


def _diff_kernel(q_ref, k_ref, v_ref, cos_ref, sin_ref, qn_ref, kn_ref, lqk_ref, sub_ref, o_ref,
                 qp_ref, kp_ref, *, seq, lambda_init):
    qi = pl.program_id(2)
    lane = lax.broadcasted_iota(jnp.int32, (1, DIFF_DV), 1)
    comp0 = lane < DIFF_DH
    in_grp = lane % DIFF_DH
    first_half = in_grp < ROPE_DIM // 2

    def prep(x, gain, cos, sin):
        x2 = x * x
        s_all = jnp.sum(x2, axis=-1, keepdims=True)
        s_lo = jnp.sum(jnp.where(comp0, x2, 0.0), axis=-1, keepdims=True)
        ms = jnp.where(comp0, s_lo, s_all - s_lo) * (1.0 / DIFF_DH)
        xn = x * lax.rsqrt(ms + EPS) * gain
        partner = jnp.where(first_half,
                            pltpu.roll(xn, DIFF_DV - ROPE_DIM // 2, axis=1),
                            pltpu.roll(xn, ROPE_DIM // 2, axis=1))
        return xn * cos + partner * sin

    @pl.when(qi == 0)
    def _():
        rows = min(DIFF_PREP_ROWS, seq)

        def body(i, carry):
            r = pl.multiple_of(i * rows, rows)
            cos = cos_ref[pl.ds(r, rows), :]
            sin = sin_ref[pl.ds(r, rows), :]
            qp = prep(q_ref[pl.ds(r, rows), :].astype(F32), qn_ref[...], cos, sin) * (DIFF_DH ** -0.5 * LOG2E)
            kp = prep(k_ref[pl.ds(r, rows), :].astype(F32), kn_ref[...], cos, sin)
            qp_ref[pl.ds(r, rows), :] = qp.astype(BF16)
            kp_ref[pl.ds(r, rows), :] = kp.astype(BF16)
            return carry

        lax.fori_loop(0, seq // rows, body, 0)

    lqk = lqk_ref[...]
    l1 = jnp.sum(lqk[0:1] * lqk[1:2], axis=-1, keepdims=True)
    l2 = jnp.sum(lqk[2:3] * lqk[3:4], axis=-1, keepdims=True)
    lam = jnp.exp(l1) - jnp.exp(l2) + lambda_init

    sb = DIFF_QB // DIFF_NSUB
    scores = []
    for j in range(DIFF_NSUB):
        r0 = pl.multiple_of(qi * DIFF_QB + j * sb, sb)
        qb = qp_ref[pl.ds(r0, sb), :]
        zero = jnp.zeros_like(qb)
        qz = jnp.concatenate([jnp.where(comp0, qb, zero), jnp.where(comp0, zero, qb)], axis=0)
        scores.append(_dot_nt(qz, kp_ref[...]))
    for j, s in enumerate(scores):
        m = jnp.max(s, axis=-1, keepdims=True)
        e = jnp.exp2(s - m)
        l = jnp.sum(e, axis=-1, keepdims=True)
        ov = _dot(e.astype(BF16), v_ref[...])
        rinv = 1.0 / l
        o = ov[0:sb] * rinv[0:sb] - ov[sb:] * (lam * rinv[sb:])
        o = _rms(o, sub_ref[...]) * (1.0 - lambda_init)
        o_ref[j * sb:(j + 1) * sb, :] = o.astype(o_ref.dtype)


def _diff(p3, cos_t, sin_t, qn, kn, lqk, sub, lambda_init):
    bsz, seq, _ = p3.shape
    qb = DIFF_QB
    return pl.pallas_call(
        functools.partial(_diff_kernel, seq=seq, lambda_init=lambda_init),
        grid=(bsz, DIFF_HEADS, seq // qb),
        in_specs=[pl.BlockSpec((None, seq, DIFF_DV), lambda b, h, i: (b, 0, C_DQ // DIFF_DV + h)),
                  pl.BlockSpec((None, seq, DIFF_DV), lambda b, h, i: (b, 0, C_DK // DIFF_DV + h)),
                  pl.BlockSpec((None, seq, DIFF_DV), lambda b, h, i: (b, 0, C_DV // DIFF_DV + h)),
                  pl.BlockSpec((seq, DIFF_DV), lambda b, h, i: (0, 0)),
                  pl.BlockSpec((seq, DIFF_DV), lambda b, h, i: (0, 0)),
                  pl.BlockSpec((1, DIFF_DV), lambda b, h, i: (0, 0)),
                  pl.BlockSpec((1, DIFF_DV), lambda b, h, i: (0, 0)),
                  pl.BlockSpec((4, DIFF_DH), lambda b, h, i: (0, 0)),
                  pl.BlockSpec((1, DIFF_DV), lambda b, h, i: (0, 0))],
        out_specs=pl.BlockSpec((None, qb, DIFF_DV), lambda b, h, i: (b, i, h)),
        out_shape=jax.ShapeDtypeStruct((bsz, seq, DIFF_V), BF16),
        scratch_shapes=[pltpu.VMEM((seq, DIFF_DV), BF16),
                        pltpu.VMEM((seq, DIFF_DV), BF16)],
        compiler_params=_cparams(("parallel", "parallel", "arbitrary")),
        name="diffattn",
    )(p3, p3, p3, cos_t, sin_t, qn, kn, lqk, sub)


SSD_XW = SSD_E * SSD_P
SSD_HALO = 8
SSD_CONV_ROWS = 256
SSD_PAR_GROUPS = 2


def _ssd_kernel(xbc_ref, sdt_ref, cw_ref, cb_ref, dtb_ref, ef_ref, eb_ref, af_ref, ab_ref, dsk_ref, o_ref,
                xpad_ref, xc_ref, dt_ref, yacc_ref, upd_ref, eac_ref, d_ref, st_ref, *, seq):
    nc = seq // CHUNK
    ng = seq // GRP_ROWS
    rows = min(SSD_CONV_ROWS, seq)
    pad = SSD_CONV // 2

    xpad_ref[0:SSD_HALO, :] = jnp.zeros((SSD_HALO, SSD_GW), F32)
    xpad_ref[seq + SSD_HALO:seq + 2 * SSD_HALO, :] = jnp.zeros((SSD_HALO, SSD_GW), F32)

    def copy_body(i, carry):
        r = pl.multiple_of(i * rows, rows)
        xpad_ref[pl.ds(r + SSD_HALO, rows), :] = xbc_ref[pl.ds(r, rows), :].astype(F32)
        dt_ref[pl.ds(r, rows), :] = _softplus(sdt_ref[pl.ds(r, rows), :].astype(F32) + dtb_ref[...])
        return carry

    lax.fori_loop(0, seq // rows, copy_body, 0)

    def conv_body(i, carry):
        r = pl.multiple_of(i * rows, rows)
        win = xpad_ref[pl.ds(r, rows + 2 * SSD_HALO), :]
        acc = cb_ref[...] + win[SSD_HALO - pad:SSD_HALO - pad + rows] * cw_ref[0:1, :]
        for w in range(1, SSD_CONV):
            acc = acc + win[SSD_HALO - pad + w:SSD_HALO - pad + w + rows] * cw_ref[w:w + 1, :]
        xc_ref[pl.ds(r, rows), :] = _silu(acc)
        return carry

    lax.fori_loop(0, seq // rows, conv_body, 0)
    row = lax.broadcasted_iota(jnp.int32, (GRP_ROWS, GRP_ROWS), 0)
    col = lax.broadcasted_iota(jnp.int32, (GRP_ROWS, GRP_ROWS), 1)
    same_chunk = row // CHUNK == col // CHUNK
    tri = (same_chunk & (col <= row)).astype(BF16)
    blk = same_chunk.astype(BF16)
    rowc = lax.broadcasted_iota(jnp.int32, (GRP_ROWS, SSD_XW), 0) % CHUNK
    colj = lax.broadcasted_iota(jnp.int32, (GRP_ROWS, SSD_XW), 1) % CHUNK
    le = colj <= rowc
    ge = colj >= rowc
    mask_ij = (le, ge)
    mask_sum = (ge, le)
    bd_r = lax.broadcasted_iota(jnp.int32, (SSD_E * CHUNK, SSD_XW), 0) // CHUNK
    bd_c = lax.broadcasted_iota(jnp.int32, (SSD_E * CHUNK, SSD_XW), 1) // SSD_P
    same_head = bd_r == bd_c
    e_refs = (ef_ref, eb_ref)
    a_refs = (af_ref, ab_ref)
    dirs = (slice(0, SSD_XW), slice(SSD_XW, 2 * SSD_XW))
    n_par = SSD_PAR_GROUPS if ng % SSD_PAR_GROUPS == 0 else 1

    def chunk_rows(g):
        return slice(g * CHUNK, (g + 1) * CHUNK)

    def decays(r, d):
        dt4 = _dot_exact_lhs(dt_ref[pl.ds(r, GRP_ROWS), :], e_refs[d][...])
        la4 = dt4 * a_refs[d][...]
        pre = _dot_exact_rhs(tri, la4)
        asum = _dot_exact_rhs(blk, jnp.where(mask_sum[d], la4, 0.0))
        return dt4, la4, pre, asum

    def phase_a(i, carry):
        ts = [i + j * (ng // n_par) for j in range(n_par)]
        rs = [pl.multiple_of(t * GRP_ROWS, GRP_ROWS) for t in ts]
        dec = [[decays(r, d) for d in range(2)] for r in rs]
        xs = [xc_ref[pl.ds(r, GRP_ROWS), 0:SSD_XW] for r in rs]
        bms = [xc_ref[pl.ds(r, GRP_ROWS), SSD_XW:SSD_XW + SSD_N].astype(BF16) for r in rs]
        cms = [xc_ref[pl.ds(r, GRP_ROWS), SSD_XW + SSD_N:SSD_XW + 2 * SSD_N].astype(BF16) for r in rs]
        cbs = [jnp.concatenate([_dot_nt(cm[chunk_rows(g)], jnp.concatenate([bm[chunk_rows(g)]] * SSD_E, axis=0))
                                for g in range(GROUP)], axis=0) for bm, cm in zip(bms, cms)]
        stage = []
        for x, cb4, dd in zip(xs, cbs, dec):
            per_dir = []
            for d in range(2):
                dt4, la4, pre, asum = dd[d]
                tot = jnp.concatenate(
                    [jnp.broadcast_to(pre[g * CHUNK + CHUNK - 1:(g + 1) * CHUNK, :], (CHUNK, SSD_XW))
                     for g in range(GROUP)], axis=0)
                acol = pre if d == 0 else tot - pre + la4
                lm = jnp.exp(jnp.where(mask_ij[d], acol - asum, -jnp.inf))
                m4 = (cb4 * lm).astype(BF16)
                xd = x * dt4
                wx = (xd * jnp.exp(tot - acol)).astype(BF16)
                per_dir.append((tot, acol, m4, xd, wx))
            stage.append(per_dir)
        for t, r, x, bm, per_dir in zip(ts, rs, xs, bms, stage):
            y = dsk_ref[...] * x
            for d in range(2):
                tot, acol, m4, xd, wx = per_dir[d]
                parts = []
                for g in range(GROUP):
                    cr = chunk_rows(g)
                    xbd = jnp.where(same_head, jnp.concatenate([xd[cr]] * SSD_E, axis=0), 0.0).astype(BF16)
                    parts.append(_dot(m4[cr], xbd))
                    upd_ref[d * nc + t * GROUP + g] = _dot_tn(bm[cr], wx[cr])
                y = y + jnp.concatenate(parts, axis=0)
                eac_ref[pl.ds(r, GRP_ROWS), dirs[d]] = jnp.exp(acol).astype(BF16)
                for g in range(GROUP):
                    d_ref[t * GROUP + g, :, dirs[d]] = jnp.exp(tot[g * CHUNK:g * CHUNK + 8, :])
            yacc_ref[pl.ds(r, GRP_ROWS), :] = y
        return carry

    lax.fori_loop(0, ng // n_par, phase_a, 0)

    st_ref[...] = jnp.zeros_like(st_ref)

    def phase_b(c, carry):
        cb = nc - 1 - c
        for d, idx, cd in ((0, c, c), (1, nc + cb, cb)):
            decay = d_ref[cd][0:1, dirs[d]]
            u = upd_ref[idx]
            s = st_ref[d]
            upd_ref[idx] = s
            st_ref[d] = s * decay + u
        return carry

    lax.fori_loop(0, nc, phase_b, 0, unroll=SCAN_UNROLL)

    def phase_c(i, carry):
        ts = [i + j * (ng // n_par) for j in range(n_par)]
        rs = [pl.multiple_of(t * GRP_ROWS, GRP_ROWS) for t in ts]
        inter = []
        for t, r in zip(ts, rs):
            cm = xc_ref[pl.ds(r, GRP_ROWS), SSD_XW + SSD_N:SSD_XW + 2 * SSD_N].astype(BF16)
            per_dir = []
            for d in range(2):
                per_dir.append(jnp.concatenate(
                    [_dot(cm[chunk_rows(g)], upd_ref[d * nc + t * GROUP + g].astype(BF16)) for g in range(GROUP)],
                    axis=0))
            inter.append(per_dir)
        for r, per_dir in zip(rs, inter):
            y = yacc_ref[pl.ds(r, GRP_ROWS), :]
            for d in range(2):
                y = y + per_dir[d] * eac_ref[pl.ds(r, GRP_ROWS), dirs[d]].astype(F32)
            o_ref[pl.ds(r, GRP_ROWS), :] = y.astype(o_ref.dtype)
        return carry

    lax.fori_loop(0, ng // n_par, phase_c, 0)


def _ssd(p3, cw, cb, dtb, ef, eb, af, ab, dsk):
    bsz, seq, _ = p3.shape
    assert seq % GRP_ROWS == 0
    nc = seq // CHUNK
    g3 = lambda shape: pl.BlockSpec((None,) + shape, lambda b, g: (g, 0, 0))
    return pl.pallas_call(
        functools.partial(_ssd_kernel, seq=seq),
        grid=(bsz, SSD_GROUPS),
        in_specs=[pl.BlockSpec((None, seq, SSD_GW), lambda b, g: (b, 0, C_SX // SSD_GW + g)),
                  pl.BlockSpec((None, seq, LANES), lambda b, g: (b, 0, C_SDT // LANES)),
                  g3((SSD_CONV, SSD_GW)), g3((1, SSD_GW)),
                  pl.BlockSpec((1, LANES), lambda b, g: (0, 0)),
                  g3((LANES, SSD_XW)), g3((LANES, SSD_XW)),
                  g3((1, SSD_XW)), g3((1, SSD_XW)), g3((1, SSD_XW))],
        out_specs=pl.BlockSpec((None, seq, SSD_XW), lambda b, g: (b, 0, g)),
        out_shape=jax.ShapeDtypeStruct((bsz, seq, SSD_INNER), BF16),
        scratch_shapes=[pltpu.VMEM((seq + 2 * SSD_HALO, SSD_GW), F32),
                        pltpu.VMEM((seq, SSD_GW), F32),
                        pltpu.VMEM((seq, LANES), F32),
                        pltpu.VMEM((seq, SSD_XW), F32),
                        pltpu.VMEM((2 * nc, SSD_N, SSD_XW), F32),
                        pltpu.VMEM((seq, 2 * SSD_XW), BF16),
                        pltpu.VMEM((nc, 8, 2 * SSD_XW), F32),
                        pltpu.VMEM((2, SSD_N, SSD_XW), F32)],
        compiler_params=_cparams(("parallel", "parallel")),
        name="ssd",
    )(p3, p3, cw, cb, dtb, ef, eb, af, ab, dsk)


MLP_TM = 512
MLP_TF = 1024


def _outmlp_kernel(x_ref, gla_ref, diff_ref, ssd_ref, z_ref, wo_ref, sn_ref, g2_ref, w1_ref, w2_ref, o_ref,
                   h_ref):
    k = pl.program_id(1)

    @pl.when(k == 0)
    def _():
        t = ssd_ref[...].astype(F32) * _silu(z_ref[...].astype(F32))
        sn = _rms(t, sn_ref[...]).astype(BF16)
        xn = (x_ref[...]
              + _dot(gla_ref[...], wo_ref[0:GLA_V, :])
              + _dot(diff_ref[...], wo_ref[GLA_V:GLA_V + DIFF_V, :])
              + _dot(sn, wo_ref[GLA_V + DIFF_V:MIX_WIDTH, :]))
        h_ref[...] = _rms(xn, g2_ref[...]).astype(BF16)
        o_ref[...] = xn

    a = _dot(h_ref[...], w1_ref[...])
    a = jnp.square(jnp.maximum(a, 0.0)).astype(BF16)
    o_ref[...] += _dot(a, w2_ref[...])


def _outmlp(x2, gla2, diff2, ssd2, p2, wo, sn, g2, w1, w2):
    t = x2.shape[0]
    tm = min(MLP_TM, t)
    row = lambda w: pl.BlockSpec((tm, w), lambda i, k: (i, 0))
    return pl.pallas_call(
        _outmlp_kernel,
        grid=(t // tm, D_FF // MLP_TF),
        in_specs=[row(D_MODEL), row(GLA_V), row(DIFF_V), row(SSD_INNER),
                  pl.BlockSpec((tm, SSD_INNER), lambda i, k: (i, C_SZ // SSD_INNER)),
                  pl.BlockSpec((MIX_WIDTH, D_MODEL), lambda i, k: (0, 0)),
                  pl.BlockSpec((1, SSD_INNER), lambda i, k: (0, 0)),
                  pl.BlockSpec((1, D_MODEL), lambda i, k: (0, 0)),
                  pl.BlockSpec((D_MODEL, MLP_TF), lambda i, k: (0, k)),
                  pl.BlockSpec((MLP_TF, D_MODEL), lambda i, k: (k, 0))],
        out_specs=pl.BlockSpec((tm, D_MODEL), lambda i, k: (i, 0)),
        out_shape=jax.ShapeDtypeStruct((t, D_MODEL), F32),
        scratch_shapes=[pltpu.VMEM((tm, D_MODEL), BF16)],
        compiler_params=_cparams(("parallel", "arbitrary")),
        name="outmlp",
    )(x2, gla2, diff2, ssd2, p2, wo, sn, g2, w1, w2)


def _prep_params(norm1, w_in, gla_wg_f, gla_bg_f, gla_wg_b, gla_bg_b, gla_norm,
                 diff_qnorm, diff_knorm, diff_lq1, diff_lk1, diff_lq2, diff_lk2, diff_subln,
                 ssd_conv_w, ssd_conv_b, ssd_dt_bias_f, ssd_dt_bias_b, ssd_A_log_f, ssd_A_log_b,
                 ssd_D, ssd_norm, w_out, norm2, w_mlp1, w_mlp2):
    depth = w_in.shape[0]
    (gq, gk, gv, gg, glr, dq, dk, dv, sz, sxbc, sdt) = jnp.split(w_in, IN_OFFSETS, axis=-1)

    def group_cols(a):
        xs, bm, cm = a[..., :SSD_INNER], a[..., SSD_INNER:SSD_INNER + SSD_BC], a[..., SSD_INNER + SSD_BC:]
        parts = []
        for g in range(SSD_GROUPS):
            parts += [xs[..., g * SSD_XW:(g + 1) * SSD_XW], bm[..., g * SSD_N:(g + 1) * SSD_N],
                      cm[..., g * SSD_N:(g + 1) * SSD_N]]
        return parts

    def pad_cols(a, width):
        return jnp.pad(a, [(0, 0)] * (a.ndim - 1) + [(0, width - a.shape[-1])])

    w_p = jnp.concatenate(group_cols(sxbc) + [gq, gk, gv, gg, dq, dk, dv,
                                              pad_cols(glr, LANES), pad_cols(sdt, LANES), sz],
                          axis=-1).astype(BF16)

    npair = GLA_HEADS // GLA_HP
    wg = jnp.zeros((depth, npair, LANES, 2 * GLA_QW), F32)
    bgs = []
    for hp in range(npair):
        cs = slice(hp * GLA_QW, (hp + 1) * GLA_QW)
        wg = wg.at[:, hp, 0:GLA_RANK, 0:GLA_QW].set(gla_wg_f[:, :, cs])
        wg = wg.at[:, hp, GLA_RANK:2 * GLA_RANK, GLA_QW:].set(gla_wg_b[:, :, cs])
        bgs.append(jnp.concatenate([gla_bg_f[:, cs], gla_bg_b[:, cs]], axis=-1))
    wg = wg.astype(BF16)
    bg = jnp.stack(bgs, axis=1)[:, :, None, :]
    gn = gla_norm[:, None, :]

    qn = jnp.tile(diff_qnorm, (1, 2))[:, None, :]
    kn = jnp.tile(diff_knorm, (1, 2))[:, None, :]
    lqk = jnp.stack([diff_lq1, diff_lk1, diff_lq2, diff_lk2], axis=1)
    sub = diff_subln[:, None, :]

    cw = jnp.concatenate(group_cols(ssd_conv_w), axis=-1).reshape(depth, SSD_CONV, SSD_GROUPS, SSD_GW)
    cw = cw.transpose(0, 2, 1, 3)
    cb = jnp.concatenate(group_cols(ssd_conv_b), axis=-1).reshape(depth, SSD_GROUPS, 1, SSD_GW)
    dtb = pad_cols(jnp.concatenate([ssd_dt_bias_f, ssd_dt_bias_b], axis=-1), LANES)[:, None, :]

    e_np = np.zeros((2, SSD_GROUPS, LANES, SSD_XW), np.float32)
    for d in range(2):
        for g in range(SSD_GROUPS):
            for e in range(SSD_E):
                e_np[d, g, d * SSD_HEADS + g * SSD_E + e, e * SSD_P:(e + 1) * SSD_P] = 1.0
    e_f = jnp.asarray(e_np[0], BF16)
    e_b = jnp.asarray(e_np[1], BF16)

    def per_lane(a):
        return jnp.repeat(a.reshape(depth, SSD_GROUPS, SSD_E), SSD_P, axis=-1)[:, :, None, :]

    a_f = per_lane(-jnp.exp(ssd_A_log_f))
    a_b = per_lane(-jnp.exp(ssd_A_log_b))
    dsk = per_lane(ssd_D)

    return dict(norm1=norm1[:, None, :], w_p=w_p, wg=wg, bg=bg, gn=gn, qn=qn, kn=kn, lqk=lqk, sub=sub,
                cw=cw, cb=cb, dtb=dtb, e_f=e_f, e_b=e_b, a_f=a_f, a_b=a_b, dsk=dsk,
                sn=ssd_norm[:, None, :], wo=w_out.astype(BF16), norm2=norm2[:, None, :],
                w1=w_mlp1.astype(BF16), w2=w_mlp2.astype(BF16))


def _rope_tables(seq):
    inv = ROPE_THETA ** (-jnp.arange(0, ROPE_DIM, 2, dtype=F32) / ROPE_DIM)
    ang = jnp.arange(seq, dtype=F32)[:, None] * inv[None, :]
    lane = np.arange(DIFF_DV) % DIFF_DH
    idx = lane % (ROPE_DIM // 2)
    in_rope = lane < ROPE_DIM
    sign = np.where(lane < ROPE_DIM // 2, -1.0, 1.0).astype(np.float32)
    cos_t = jnp.where(in_rope[None, :], jnp.cos(ang)[:, idx], 1.0)
    sin_t = jnp.where(in_rope[None, :], jnp.sin(ang)[:, idx] * sign[None, :], 0.0)
    return cos_t.astype(F32), sin_t.astype(F32)


def _trunk(x, pr):
    bsz, seq, _ = x.shape
    cos_t, sin_t = _rope_tables(seq)
    x2 = x.reshape(bsz * seq, D_MODEL)
    for l in range(DEPTH):
        p2 = _inproj(x2, pr["norm1"][l], pr["w_p"][l])
        p3 = p2.reshape(bsz, seq, P_WIDTH)
        o_gla = _gla(p3, pr["wg"][l], pr["bg"][l], pr["gn"][l])
        lambda_init = 0.8 - 0.6 * math.exp(-0.3 * l)
        o_diff = _diff(p3, cos_t, sin_t, pr["qn"][l], pr["kn"][l], pr["lqk"][l], pr["sub"][l], lambda_init)
        o_ssd = _ssd(p3, pr["cw"][l], pr["cb"][l], pr["dtb"][l], pr["e_f"], pr["e_b"],
                     pr["a_f"][l], pr["a_b"][l], pr["dsk"][l])
        x2 = _outmlp(x2, o_gla.reshape(bsz * seq, GLA_V), o_diff.reshape(bsz * seq, DIFF_V),
                     o_ssd.reshape(bsz * seq, SSD_INNER), p2, pr["wo"][l], pr["sn"][l], pr["norm2"][l],
                     pr["w1"][l], pr["w2"][l])
    return x2.reshape(bsz, seq, D_MODEL)


def kernel(x_prompt, x_sample, norm1, w_in, gla_wg_f, gla_bg_f, gla_wg_b, gla_bg_b, gla_norm, diff_qnorm, diff_knorm, diff_lq1, diff_lk1, diff_lq2, diff_lk2, diff_subln, ssd_conv_w, ssd_conv_b, ssd_dt_bias_f, ssd_dt_bias_b, ssd_A_log_f, ssd_A_log_b, ssd_D, ssd_norm, w_out, norm2, w_mlp1, w_mlp2):
    pr = _prep_params(norm1, w_in, gla_wg_f, gla_bg_f, gla_wg_b, gla_bg_b, gla_norm,
                      diff_qnorm, diff_knorm, diff_lq1, diff_lk1, diff_lq2, diff_lk2, diff_subln,
                      ssd_conv_w, ssd_conv_b, ssd_dt_bias_f, ssd_dt_bias_b, ssd_A_log_f, ssd_A_log_b,
                      ssd_D, ssd_norm, w_out, norm2, w_mlp1, w_mlp2)
    return (_trunk(x_prompt, pr), _trunk(x_sample, pr))
```

```python
import functools
import math

import numpy as np
import jax
import jax.numpy as jnp
from jax import lax
from jax.experimental import pallas as pl
from jax.experimental.pallas import tpu as pltpu

F32 = jnp.float32
BF16 = jnp.bfloat16

D_MODEL = 1024
DEPTH = 4
GLA_HEADS = 4
GLA_DK = 64
GLA_DV = 128
GLA_RANK = 16
GLA_GATE_NORM = 16.0
DIFF_HEADS = 4
DIFF_DH = 64
DIFF_DV = 2 * DIFF_DH
ROPE_THETA = 500000.0
ROPE_DIM = DIFF_DH // 4
SSD_HEADS = 8
SSD_P = 64
SSD_GROUPS = 2
SSD_E = SSD_HEADS // SSD_GROUPS
SSD_N = 64
SSD_CONV = 5
CHUNK = 64
D_FF = 4 * D_MODEL
EPS = 1e-6

GLA_QK = GLA_HEADS * GLA_DK
GLA_V = GLA_HEADS * GLA_DV
DIFF_QK = DIFF_HEADS * 2 * DIFF_DH
DIFF_V = DIFF_HEADS * DIFF_DV
SSD_INNER = SSD_HEADS * SSD_P
SSD_BC = SSD_GROUPS * SSD_N
SSD_XBC = SSD_INNER + 2 * SSD_BC
MIX_WIDTH = GLA_V + DIFF_V + SSD_INNER
IN_SIZES = (GLA_QK, GLA_QK, GLA_V, GLA_V, 2 * GLA_RANK,
            DIFF_QK, DIFF_QK, DIFF_V,
            SSD_INNER, SSD_XBC, 2 * SSD_HEADS)
IN_OFFSETS = tuple(int(v) for v in np.cumsum(IN_SIZES)[:-1])

LANES = 128

SSD_GW = SSD_E * SSD_P + 2 * SSD_N
C_SX = 0
C_GQ = C_SX + SSD_GROUPS * SSD_GW
C_GK = C_GQ + GLA_QK
C_GV = C_GK + GLA_QK
C_GG = C_GV + GLA_V
C_DQ = C_GG + GLA_V
C_DK = C_DQ + DIFF_QK
C_DV = C_DK + DIFF_QK
C_GLR = C_DV + DIFF_V
C_SDT = C_GLR + LANES
C_SZ = C_SDT + LANES
P_WIDTH = C_SZ + SSD_INNER

VMEM_LIMIT = 56 * 1024 * 1024
SCAN_UNROLL = 2
GROUP = 4
GRP_ROWS = GROUP * CHUNK
PAR_GROUPS = 4


def _cparams(sem):
    return pltpu.CompilerParams(dimension_semantics=sem, vmem_limit_bytes=VMEM_LIMIT)


def _dot(a, b):
    return jnp.dot(a, b, preferred_element_type=F32)


def _dot_nt(a, b):
    return lax.dot_general(a, b, (((1,), (1,)), ((), ())), preferred_element_type=F32)


def _dot_tn(a, b):
    return lax.dot_general(a, b, (((0,), (0,)), ((), ())), preferred_element_type=F32)


def _split(x):
    hi = x.astype(BF16)
    lo = (x - hi.astype(F32)).astype(BF16)
    return hi, lo


def _dot_exact_rhs(a01, x):
    hi, lo = _split(x)
    return _dot(a01, hi) + _dot(a01, lo)


def _dot_exact_lhs(x, b01):
    hi, lo = _split(x)
    return _dot(hi, b01) + _dot(lo, b01)


def _dot_select(sel01, x):
    hi = x.astype(BF16)
    r1 = x - hi.astype(F32)
    mid = r1.astype(BF16)
    lo = (r1 - mid.astype(F32)).astype(BF16)
    return _dot(sel01, hi) + _dot(sel01, mid) + _dot(sel01, lo)


def _silu(x):
    return x * (1.0 / (1.0 + jnp.exp(-x)))


def _softplus(x):
    return jnp.maximum(x, 0.0) + jnp.log(1.0 + jnp.exp(-jnp.abs(x)))


def _rms(x, g):
    ms = jnp.mean(x * x, axis=-1, keepdims=True)
    return x * lax.rsqrt(ms + EPS) * g


IN_TM = 512
IN_TN = 512


def _inproj_kernel(x_ref, g_ref, w_ref, o_ref):
    h = _rms(x_ref[...], g_ref[...]).astype(BF16)
    for j in range(P_WIDTH // IN_TN):
        o_ref[:, j * IN_TN:(j + 1) * IN_TN] = _dot(h, w_ref[:, j * IN_TN:(j + 1) * IN_TN]).astype(BF16)


def _inproj(x2, g, w):
    t = x2.shape[0]
    tm = min(IN_TM, t)
    return pl.pallas_call(
        _inproj_kernel,
        grid=(t // tm,),
        in_specs=[pl.BlockSpec((tm, D_MODEL), lambda i: (i, 0)),
                  pl.BlockSpec((1, D_MODEL), lambda i: (0, 0)),
                  pl.BlockSpec((D_MODEL, P_WIDTH), lambda i: (0, 0))],
        out_specs=pl.BlockSpec((tm, P_WIDTH), lambda i: (i, 0)),
        out_shape=jax.ShapeDtypeStruct((t, P_WIDTH), BF16),
        compiler_params=_cparams(("parallel",)),
        name="inproj",
    )(x2, g, w)


GLA_HP = 2
GLA_MAX_LOGDECAY = 60.0
GLA_QW = GLA_HP * GLA_DK
GLA_VW = GLA_HP * GLA_DV


def _gla_kernel(q_ref, k_ref, v_ref, gg_ref, glr_ref, wg_ref, bg_ref, gn_ref, o_ref,
                upd_ref, qe_ref, oacc_ref, d_ref, st_ref, *, seq):
    nc = seq // CHUNK
    ng = seq // GRP_ROWS
    scale = GLA_DK ** -0.5

    row = lax.broadcasted_iota(jnp.int32, (GRP_ROWS, GRP_ROWS), 0)
    col = lax.broadcasted_iota(jnp.int32, (GRP_ROWS, GRP_ROWS), 1)
    tri = ((row // CHUNK == col // CHUNK) & (col <= row)).astype(BF16)
    row2 = lax.broadcasted_iota(jnp.int32, (GLA_HP * GRP_ROWS, GRP_ROWS), 0) % GRP_ROWS
    col2 = lax.broadcasted_iota(jnp.int32, (GLA_HP * GRP_ROWS, GRP_ROWS), 1)
    same_chunk = row2 // CHUNK == col2 // CHUNK
    masks = (same_chunk & (col2 <= row2), same_chunk & (col2 >= row2))
    head0 = lax.broadcasted_iota(jnp.int32, (GRP_ROWS, GLA_QW), 1) < GLA_DK
    is_fwd = lax.broadcasted_iota(jnp.int32, (GRP_ROWS, 2 * GLA_QW), 1) < GLA_QW
    st_r = lax.broadcasted_iota(jnp.int32, (GLA_VW, GLA_QW), 0) // GLA_DV
    st_c = lax.broadcasted_iota(jnp.int32, (GLA_VW, GLA_QW), 1) // GLA_DK
    same_head = st_r == st_c

    n_par = PAR_GROUPS if ng % PAR_GROUPS == 0 else 1
    dirs = (slice(0, GLA_QW), slice(GLA_QW, 2 * GLA_QW))

    def gates(r):
        x = _dot(glr_ref[pl.ds(r, GRP_ROWS), :], wg_ref[...]) + bg_ref[...]
        return -_softplus(-x) * (1.0 / GLA_GATE_NORM)

    def decayed(r, lg, pre):
        tot = jnp.concatenate(
            [jnp.broadcast_to(pre[g * CHUNK + CHUNK - 1:(g + 1) * CHUNK, :], (CHUNK, 2 * GLA_QW))
             for g in range(GROUP)], axis=0)
        b = jnp.where(is_fwd, pre, tot - pre + lg)
        q = q_ref[pl.ds(r, GRP_ROWS), :].astype(F32)
        k = k_ref[pl.ds(r, GRP_ROWS), :].astype(F32)
        qe = jnp.concatenate([q, q], axis=1) * jnp.exp(b) * scale
        kk = jnp.concatenate([k, k], axis=1)
        ke = (kk * jnp.exp(-b)).astype(BF16)
        kl = (kk * jnp.exp(tot - b)).astype(BF16)
        q2 = [by_head(qe[:, sl]) for sl in dirs]
        return tot, qe.astype(BF16), ke, kl, q2, (b, q, k)

    def by_head(x):
        return jnp.concatenate([jnp.where(head0, x, 0.0), jnp.where(head0, 0.0, x)], axis=0).astype(BF16)

    def steep_scores(d, b, q, k):
        pi = row2 % CHUNK
        pj = col2 % CHUNK
        total = jnp.where(same_chunk & (pi == pj), _dot_nt(by_head(q * scale), k.astype(BF16)), 0.0)
        srow = row % CHUNK
        s = CHUNK // 2
        while s >= 1:
            blk_i, blk_j = pi // s, pj // s
            sib = same_chunk & (blk_i // 2 == blk_j // 2)
            if d == 0:
                pair = sib & (blk_i % 2 == 1) & (blk_j % 2 == 0)
                sel_q = (srow // s >= 1) & (col == row - srow % s - 1)
                sel_k = col == row - srow % s + s - 1
            else:
                pair = sib & (blk_i % 2 == 0) & (blk_j % 2 == 1)
                sel_q = (srow // s < CHUNK // s - 1) & (col == row - srow % s + s)
                sel_k = col == row - srow % s
            ref_q = _dot_select(sel_q.astype(BF16), b)
            ref_k = _dot_select(sel_k.astype(BF16), b)
            qt = q * jnp.exp(b - ref_q) * scale
            kt = (k * jnp.exp(ref_k - b)).astype(BF16)
            total = total + jnp.where(pair, _dot_nt(by_head(qt), kt), 0.0)
            s //= 2
        return total

    def phase_a(i, carry):
        ts = [i + j * (ng // n_par) for j in range(n_par)]
        rs = [pl.multiple_of(t * GRP_ROWS, GRP_ROWS) for t in ts]
        lgs = [gates(r) for r in rs]
        steep = functools.reduce(jnp.maximum, [jnp.max(-lg) for lg in lgs]) * CHUNK > GLA_MAX_LOGDECAY

        @pl.when(steep)
        def _():
            for t, r, lg in zip(ts, rs, lgs):
                phase_a_rest([t], [r], [lg], True)

        @pl.when(jnp.logical_not(steep))
        def _():
            phase_a_rest(ts, rs, lgs, False)

        return carry

    def phase_a_rest(ts, rs, lgs, steep):
        pres = [_dot_exact_rhs(tri, lg) for lg in lgs]
        dec = [decayed(r, lg, pre) for r, lg, pre in zip(rs, lgs, pres)]
        vs = [v_ref[pl.ds(r, GRP_ROWS), :] for r in rs]
        if steep:
            scores = [[steep_scores(d, b[:, dirs[d]], q, k) for d in range(2)] for (*_, (b, q, k)) in dec]
        else:
            scores = [[_dot_nt(q2[d], ke[:, dirs[d]]) for d in range(2)] for (_, _, ke, _, q2, _) in dec]
        upds = [[[_dot_tn(v[g * CHUNK:(g + 1) * CHUNK], kl[g * CHUNK:(g + 1) * CHUNK, dirs[d]])
                  for g in range(GROUP)] for d in range(2)] for v, (_, _, _, kl, _, _) in zip(vs, dec)]
        for t, r, v, (tot, qe, _, _, _, _), sc, up in zip(ts, rs, vs, dec, scores, upds):
            qe_ref[pl.ds(r, GRP_ROWS), :] = qe
            o_sum = None
            for d in range(2):
                a = jnp.where(masks[d], sc[d], 0.0).astype(BF16)
                o_d = jnp.concatenate(
                    [_dot(a[h * GRP_ROWS:(h + 1) * GRP_ROWS], v[:, h * GLA_DV:(h + 1) * GLA_DV])
                     for h in range(GLA_HP)], axis=1)
                o_sum = o_d if o_sum is None else o_sum + o_d
                for g in range(GROUP):
                    upd_ref[d * nc + t * GROUP + g] = jnp.where(same_head, up[d][g], 0.0)
            oacc_ref[pl.ds(r, GRP_ROWS), :] = o_sum
            for g in range(GROUP):
                d_ref[t * GROUP + g] = jnp.exp(tot[g * CHUNK:g * CHUNK + 8, :])

    lax.fori_loop(0, ng // n_par, phase_a, 0)

    st_ref[...] = jnp.zeros_like(st_ref)

    def phase_b(c, carry):
        cb = nc - 1 - c
        for d, idx, cd in ((0, c, c), (1, nc + cb, cb)):
            dec = d_ref[cd][0:1, d * GLA_QW:(d + 1) * GLA_QW]
            u = upd_ref[idx]
            s = st_ref[d]
            upd_ref[idx] = s
            st_ref[d] = s * dec + u
        return carry

    lax.fori_loop(0, nc, phase_b, 0, unroll=SCAN_UNROLL)

    def phase_c(i, carry):
        ts = [i + j * (ng // n_par) for j in range(n_par)]
        rs = [pl.multiple_of(t * GRP_ROWS, GRP_ROWS) for t in ts]
        inter = []
        for t, r in zip(ts, rs):
            parts = []
            for g in range(GROUP):
                qe = qe_ref[pl.ds(r + g * CHUNK, CHUNK), :]
                ods = [_dot_nt(qe[:, dirs[d]], upd_ref[d * nc + t * GROUP + g].astype(BF16)) for d in range(2)]
                parts.append(ods[0] + ods[1])
            inter.append(parts)
        for r, parts in zip(rs, inter):
            o = oacc_ref[pl.ds(r, GRP_ROWS), :] + jnp.concatenate(parts, axis=0)
            gg = gg_ref[pl.ds(r, GRP_ROWS), :].astype(F32)
            normed = [_rms(o[:, h * GLA_DV:(h + 1) * GLA_DV], gn_ref[...]) for h in range(GLA_HP)]
            y = jnp.concatenate(normed, axis=1) * _silu(gg)
            o_ref[pl.ds(r, GRP_ROWS), :] = y.astype(o_ref.dtype)
        return carry

    lax.fori_loop(0, ng // n_par, phase_c, 0)


def _gla(p3, wg, bg, gn):
    bsz, seq, _ = p3.shape
    assert seq % GRP_ROWS == 0
    nc = seq // CHUNK
    npair = GLA_HEADS // GLA_HP
    return pl.pallas_call(
        functools.partial(_gla_kernel, seq=seq),
        grid=(bsz, npair),
        in_specs=[pl.BlockSpec((None, seq, GLA_QW), lambda b, h: (b, 0, C_GQ // GLA_QW + h)),
                  pl.BlockSpec((None, seq, GLA_QW), lambda b, h: (b, 0, C_GK // GLA_QW + h)),
                  pl.BlockSpec((None, seq, GLA_VW), lambda b, h: (b, 0, C_GV // GLA_VW + h)),
                  pl.BlockSpec((None, seq, GLA_VW), lambda b, h: (b, 0, C_GG // GLA_VW + h)),
                  pl.BlockSpec((None, seq, LANES), lambda b, h: (b, 0, C_GLR // LANES)),
                  pl.BlockSpec((None, LANES, 2 * GLA_QW), lambda b, h: (h, 0, 0)),
                  pl.BlockSpec((None, 1, 2 * GLA_QW), lambda b, h: (h, 0, 0)),
                  pl.BlockSpec((1, GLA_DV), lambda b, h: (0, 0))],
        out_specs=pl.BlockSpec((None, seq, GLA_VW), lambda b, h: (b, 0, h)),
        out_shape=jax.ShapeDtypeStruct((bsz, seq, GLA_V), BF16),
        scratch_shapes=[pltpu.VMEM((2 * nc, GLA_VW, GLA_QW), F32),
                        pltpu.VMEM((seq, 2 * GLA_QW), BF16),
                        pltpu.VMEM((seq, GLA_VW), F32),
                        pltpu.VMEM((nc, 8, 2 * GLA_QW), F32),
                        pltpu.VMEM((2, GLA_VW, GLA_QW), F32)],
        compiler_params=_cparams(("parallel", "parallel")),
        name="gla",
    )(p3, p3, p3, p3, p3, wg, bg, gn)


DIFF_QB = 512
DIFF_NSUB = 4
DIFF_TK = 256
DIFF_PREP_ROWS = 256
DIFF_PREP_TILES = 2
LOG2E = math.log2(math.e)
DIFF_MAX_SHIFT = 60.0


def _diff_kernel(shift_ref, q_ref, k_ref, v_ref, cos_ref, sin_ref, qn_ref, kn_ref, lqk_ref, sub_ref, o_ref,
                 qp_ref, kp_ref, *, seq, lambda_init):
    qi = pl.program_id(2)
    lane = lax.broadcasted_iota(jnp.int32, (1, DIFF_DV), 1)
    comp0 = lane < DIFF_DH

    @pl.when(qi == 0)
    def _():
        rows = min(DIFF_PREP_ROWS, seq)
        src = lax.broadcasted_iota(jnp.int32, (DIFF_DV, DIFF_DV), 0)
        dst = lax.broadcasted_iota(jnp.int32, (DIFF_DV, DIFF_DV), 1)
        same_comp = (src // DIFF_DH == dst // DIFF_DH).astype(BF16)
        half = ROPE_DIM // 2
        dpos = dst % DIFF_DH
        partner_of = ((dpos < half) & (src == dst + half)) | ((dpos >= half) & (dpos < ROPE_DIM) & (src == dst - half))
        partner_of = partner_of.astype(BF16)

        tiles = DIFF_PREP_TILES if (seq // rows) % DIFF_PREP_TILES == 0 else 1
        work = ((q_ref, qn_ref, qp_ref, DIFF_DH ** -0.5 * LOG2E), (k_ref, kn_ref, kp_ref, 1.0))

        def body(i, carry):
            rs = [pl.multiple_of((i * tiles + j) * rows, rows) for j in range(tiles)]
            items = [(src_ref, g_ref, dst_ref, mul, r) for r in rs for (src_ref, g_ref, dst_ref, mul) in work]
            xs = [src_ref[pl.ds(r, rows), :].astype(F32) for (src_ref, _, _, _, r) in items]
            ms = [_dot_exact_lhs(x * x, same_comp) * (1.0 / DIFF_DH) for x in xs]
            xn = [x * lax.rsqrt(m + EPS) * it[1][...] for x, m, it in zip(xs, ms, items)]
            partner = [_dot_exact_lhs(v, partner_of) for v in xn]
            for v, p, (_, _, dst_ref, mul, r) in zip(xn, partner, items):
                out = v * cos_ref[pl.ds(r, rows), :] + p * sin_ref[pl.ds(r, rows), :]
                dst_ref[pl.ds(r, rows), :] = (out * mul).astype(BF16)
            return carry

        lax.fori_loop(0, seq // (rows * tiles), body, 0)

    lqk = lqk_ref[...]
    l1 = jnp.sum(lqk[0:1] * lqk[1:2], axis=-1, keepdims=True)
    l2 = jnp.sum(lqk[2:3] * lqk[3:4], axis=-1, keepdims=True)
    lam = jnp.exp(l1) - jnp.exp(l2) + lambda_init

    def stacked_q(r0, n):
        qb = qp_ref[pl.ds(r0, n), :]
        zero = jnp.zeros_like(qb)
        return jnp.concatenate([jnp.where(comp0, qb, zero), jnp.where(comp0, zero, qb)], axis=0)

    def finish(rows, n, ov, l):
        rinv = 1.0 / l
        o = ov[0:n] * rinv[0:n] - ov[n:] * (lam * rinv[n:])
        o = _rms(o, sub_ref[...]) * (1.0 - lambda_init)
        o_ref[rows, :] = o.astype(o_ref.dtype)

    shift = shift_ref[0]

    @pl.when(shift <= DIFF_MAX_SHIFT)
    def _():
        qz = stacked_q(pl.multiple_of(qi * DIFF_QB, DIFF_QB), DIFF_QB)
        nkt = seq // DIFF_TK
        lacc = jnp.zeros((2 * DIFF_QB, LANES), F32)
        ov = jnp.zeros((2 * DIFF_QB, DIFF_DV), F32)
        s_prev = None
        for j in range(nkt + 1):
            s_cur = _dot_nt(qz, kp_ref[j * DIFF_TK:(j + 1) * DIFF_TK, :]) if j < nkt else None
            if s_prev is not None:
                e = jnp.exp2(s_prev - shift)
                for c in range(DIFF_TK // LANES):
                    lacc = lacc + e[:, c * LANES:(c + 1) * LANES]
                ov = ov + _dot(e.astype(BF16), v_ref[(j - 1) * DIFF_TK:j * DIFF_TK, :])
            s_prev = s_cur
        finish(slice(0, DIFF_QB), DIFF_QB, ov, jnp.sum(lacc, axis=-1, keepdims=True))

    @pl.when(shift > DIFF_MAX_SHIFT)
    def _():
        sb = DIFF_QB // DIFF_NSUB
        scores = [_dot_nt(stacked_q(pl.multiple_of(qi * DIFF_QB + j * sb, sb), sb), kp_ref[...])
                  for j in range(DIFF_NSUB)]
        for j, s in enumerate(scores):
            e = jnp.exp2(s - jnp.max(s, axis=-1, keepdims=True))
            finish(slice(j * sb, (j + 1) * sb), sb, _dot(e.astype(BF16), v_ref[...]),
                   jnp.sum(e, axis=-1, keepdims=True))


def _diff(p3, cos_t, sin_t, shift, qn, kn, lqk, sub, lambda_init):
    bsz, seq, _ = p3.shape
    qb = DIFF_QB
    assert seq % qb == 0 and seq % DIFF_TK == 0
    return pl.pallas_call(
        functools.partial(_diff_kernel, seq=seq, lambda_init=lambda_init),
        grid=(bsz, DIFF_HEADS, seq // qb),
        in_specs=[pl.BlockSpec(memory_space=pltpu.SMEM),
                  pl.BlockSpec((None, seq, DIFF_DV), lambda b, h, i: (b, 0, C_DQ // DIFF_DV + h)),
                  pl.BlockSpec((None, seq, DIFF_DV), lambda b, h, i: (b, 0, C_DK // DIFF_DV + h)),
                  pl.BlockSpec((None, seq, DIFF_DV), lambda b, h, i: (b, 0, C_DV // DIFF_DV + h)),
                  pl.BlockSpec((seq, DIFF_DV), lambda b, h, i: (0, 0)),
                  pl.BlockSpec((seq, DIFF_DV), lambda b, h, i: (0, 0)),
                  pl.BlockSpec((1, DIFF_DV), lambda b, h, i: (0, 0)),
                  pl.BlockSpec((1, DIFF_DV), lambda b, h, i: (0, 0)),
                  pl.BlockSpec((4, DIFF_DH), lambda b, h, i: (0, 0)),
                  pl.BlockSpec((1, DIFF_DV), lambda b, h, i: (0, 0))],
        out_specs=pl.BlockSpec((None, qb, DIFF_DV), lambda b, h, i: (b, i, h)),
        out_shape=jax.ShapeDtypeStruct((bsz, seq, DIFF_V), BF16),
        scratch_shapes=[pltpu.VMEM((seq, DIFF_DV), BF16),
                        pltpu.VMEM((seq, DIFF_DV), BF16)],
        compiler_params=_cparams(("parallel", "parallel", "arbitrary")),
        name="diffattn",
    )(shift, p3, p3, p3, cos_t, sin_t, qn, kn, lqk, sub)


SSD_XW = SSD_E * SSD_P
SSD_HALO = 8
SSD_CONV_ROWS = 256
SSD_PAR_GROUPS = 2


def _ssd_kernel(xbc_ref, sdt_ref, cw_ref, cb_ref, dtb_ref, ef_ref, eb_ref, af_ref, ab_ref, dsk_ref, o_ref,
                xpad_ref, xc_ref, dt_ref, yacc_ref, upd_ref, eac_ref, d_ref, st_ref, *, seq):
    nc = seq // CHUNK
    ng = seq // GRP_ROWS
    rows = min(SSD_CONV_ROWS, seq)
    pad = SSD_CONV // 2

    xpad_ref[0:SSD_HALO, :] = jnp.zeros((SSD_HALO, SSD_GW), F32)
    xpad_ref[seq + SSD_HALO:seq + 2 * SSD_HALO, :] = jnp.zeros((SSD_HALO, SSD_GW), F32)

    def copy_body(i, carry):
        r = pl.multiple_of(i * rows, rows)
        xpad_ref[pl.ds(r + SSD_HALO, rows), :] = xbc_ref[pl.ds(r, rows), :].astype(F32)
        dt_ref[pl.ds(r, rows), :] = _softplus(sdt_ref[pl.ds(r, rows), :].astype(F32) + dtb_ref[...])
        return carry

    lax.fori_loop(0, seq // rows, copy_body, 0)

    def conv_body(i, carry):
        r = pl.multiple_of(i * rows, rows)
        win = xpad_ref[pl.ds(r, rows + 2 * SSD_HALO), :]
        acc = cb_ref[...] + win[SSD_HALO - pad:SSD_HALO - pad + rows] * cw_ref[0:1, :]
        for w in range(1, SSD_CONV):
            acc = acc + win[SSD_HALO - pad + w:SSD_HALO - pad + w + rows] * cw_ref[w:w + 1, :]
        xc_ref[pl.ds(r, rows), :] = _silu(acc)
        return carry

    lax.fori_loop(0, seq // rows, conv_body, 0)
    row = lax.broadcasted_iota(jnp.int32, (GRP_ROWS, GRP_ROWS), 0)
    col = lax.broadcasted_iota(jnp.int32, (GRP_ROWS, GRP_ROWS), 1)
    same_chunk = row // CHUNK == col // CHUNK
    tri = (same_chunk & (col <= row)).astype(BF16)
    blk = same_chunk.astype(BF16)
    rowc = lax.broadcasted_iota(jnp.int32, (GRP_ROWS, SSD_XW), 0) % CHUNK
    colj = lax.broadcasted_iota(jnp.int32, (GRP_ROWS, SSD_XW), 1) % CHUNK
    le = colj <= rowc
    ge = colj >= rowc
    mask_ij = (le, ge)
    mask_sum = (ge, le)
    bd_r = lax.broadcasted_iota(jnp.int32, (SSD_E * CHUNK, SSD_XW), 0) // CHUNK
    bd_c = lax.broadcasted_iota(jnp.int32, (SSD_E * CHUNK, SSD_XW), 1) // SSD_P
    same_head = bd_r == bd_c
    e_refs = (ef_ref, eb_ref)
    a_refs = (af_ref, ab_ref)
    dirs = (slice(0, SSD_XW), slice(SSD_XW, 2 * SSD_XW))
    n_par = SSD_PAR_GROUPS if ng % SSD_PAR_GROUPS == 0 else 1

    def chunk_rows(g):
        return slice(g * CHUNK, (g + 1) * CHUNK)

    def decays(r, d):
        dt4 = _dot_exact_lhs(dt_ref[pl.ds(r, GRP_ROWS), :], e_refs[d][...])
        la4 = dt4 * a_refs[d][...]
        pre = _dot_exact_rhs(tri, la4)
        asum = _dot_exact_rhs(blk, jnp.where(mask_sum[d], la4, 0.0))
        return dt4, la4, pre, asum

    def phase_a(i, carry):
        ts = [i + j * (ng // n_par) for j in range(n_par)]
        rs = [pl.multiple_of(t * GRP_ROWS, GRP_ROWS) for t in ts]
        dec = [[decays(r, d) for d in range(2)] for r in rs]
        xs = [xc_ref[pl.ds(r, GRP_ROWS), 0:SSD_XW] for r in rs]
        bms = [xc_ref[pl.ds(r, GRP_ROWS), SSD_XW:SSD_XW + SSD_N].astype(BF16) for r in rs]
        cms = [xc_ref[pl.ds(r, GRP_ROWS), SSD_XW + SSD_N:SSD_XW + 2 * SSD_N].astype(BF16) for r in rs]
        cbs = [jnp.concatenate([_dot_nt(cm[chunk_rows(g)], jnp.concatenate([bm[chunk_rows(g)]] * SSD_E, axis=0))
                                for g in range(GROUP)], axis=0) for bm, cm in zip(bms, cms)]
        stage = []
        for x, cb4, dd in zip(xs, cbs, dec):
            per_dir = []
            for d in range(2):
                dt4, la4, pre, asum = dd[d]
                tot = jnp.concatenate(
                    [jnp.broadcast_to(pre[g * CHUNK + CHUNK - 1:(g + 1) * CHUNK, :], (CHUNK, SSD_XW))
                     for g in range(GROUP)], axis=0)
                acol = pre if d == 0 else tot - pre + la4
                lm = jnp.exp(jnp.where(mask_ij[d], acol - asum, -jnp.inf))
                m4 = (cb4 * lm).astype(BF16)
                xd = x * dt4
                wx = (xd * jnp.exp(tot - acol)).astype(BF16)
                per_dir.append((tot, acol, m4, xd, wx))
            stage.append(per_dir)
        for t, r, x, bm, per_dir in zip(ts, rs, xs, bms, stage):
            y = dsk_ref[...] * x
            for d in range(2):
                tot, acol, m4, xd, wx = per_dir[d]
                parts = []
                for g in range(GROUP):
                    cr = chunk_rows(g)
                    xbd = jnp.where(same_head, jnp.concatenate([xd[cr]] * SSD_E, axis=0), 0.0).astype(BF16)
                    parts.append(_dot(m4[cr], xbd))
                    upd_ref[d * nc + t * GROUP + g] = _dot_tn(bm[cr], wx[cr])
                y = y + jnp.concatenate(parts, axis=0)
                eac_ref[pl.ds(r, GRP_ROWS), dirs[d]] = jnp.exp(acol).astype(BF16)
                for g in range(GROUP):
                    d_ref[t * GROUP + g, :, dirs[d]] = jnp.exp(tot[g * CHUNK:g * CHUNK + 8, :])
            yacc_ref[pl.ds(r, GRP_ROWS), :] = y
        return carry

    lax.fori_loop(0, ng // n_par, phase_a, 0)

    st_ref[...] = jnp.zeros_like(st_ref)

    def phase_b(c, carry):
        cb = nc - 1 - c
        for d, idx, cd in ((0, c, c), (1, nc + cb, cb)):
            decay = d_ref[cd][0:1, dirs[d]]
            u = upd_ref[idx]
            s = st_ref[d]
            upd_ref[idx] = s
            st_ref[d] = s * decay + u
        return carry

    lax.fori_loop(0, nc, phase_b, 0, unroll=SCAN_UNROLL)

    def phase_c(i, carry):
        ts = [i + j * (ng // n_par) for j in range(n_par)]
        rs = [pl.multiple_of(t * GRP_ROWS, GRP_ROWS) for t in ts]
        inter = []
        for t, r in zip(ts, rs):
            cm = xc_ref[pl.ds(r, GRP_ROWS), SSD_XW + SSD_N:SSD_XW + 2 * SSD_N].astype(BF16)
            per_dir = []
            for d in range(2):
                per_dir.append(jnp.concatenate(
                    [_dot(cm[chunk_rows(g)], upd_ref[d * nc + t * GROUP + g].astype(BF16)) for g in range(GROUP)],
                    axis=0))
            inter.append(per_dir)
        for r, per_dir in zip(rs, inter):
            y = yacc_ref[pl.ds(r, GRP_ROWS), :]
            for d in range(2):
                y = y + per_dir[d] * eac_ref[pl.ds(r, GRP_ROWS), dirs[d]].astype(F32)
            o_ref[pl.ds(r, GRP_ROWS), :] = y.astype(o_ref.dtype)
        return carry

    lax.fori_loop(0, ng // n_par, phase_c, 0)


def _ssd(p3, cw, cb, dtb, ef, eb, af, ab, dsk):
    bsz, seq, _ = p3.shape
    assert seq % GRP_ROWS == 0
    nc = seq // CHUNK
    g3 = lambda shape: pl.BlockSpec((None,) + shape, lambda b, g: (g, 0, 0))
    return pl.pallas_call(
        functools.partial(_ssd_kernel, seq=seq),
        grid=(bsz, SSD_GROUPS),
        in_specs=[pl.BlockSpec((None, seq, SSD_GW), lambda b, g: (b, 0, C_SX // SSD_GW + g)),
                  pl.BlockSpec((None, seq, LANES), lambda b, g: (b, 0, C_SDT // LANES)),
                  g3((SSD_CONV, SSD_GW)), g3((1, SSD_GW)),
                  pl.BlockSpec((1, LANES), lambda b, g: (0, 0)),
                  g3((LANES, SSD_XW)), g3((LANES, SSD_XW)),
                  g3((1, SSD_XW)), g3((1, SSD_XW)), g3((1, SSD_XW))],
        out_specs=pl.BlockSpec((None, seq, SSD_XW), lambda b, g: (b, 0, g)),
        out_shape=jax.ShapeDtypeStruct((bsz, seq, SSD_INNER), BF16),
        scratch_shapes=[pltpu.VMEM((seq + 2 * SSD_HALO, SSD_GW), F32),
                        pltpu.VMEM((seq, SSD_GW), F32),
                        pltpu.VMEM((seq, LANES), F32),
                        pltpu.VMEM((seq, SSD_XW), F32),
                        pltpu.VMEM((2 * nc, SSD_N, SSD_XW), F32),
                        pltpu.VMEM((seq, 2 * SSD_XW), BF16),
                        pltpu.VMEM((nc, 8, 2 * SSD_XW), F32),
                        pltpu.VMEM((2, SSD_N, SSD_XW), F32)],
        compiler_params=_cparams(("parallel", "parallel")),
        name="ssd",
    )(p3, p3, cw, cb, dtb, ef, eb, af, ab, dsk)


MLP_TM = 512
MLP_TF = 1024


def _outmlp_kernel(x_ref, gla_ref, diff_ref, ssd_ref, z_ref, wo_ref, sn_ref, g2_ref, w1_ref, w2_ref, o_ref):
    t = ssd_ref[...].astype(F32) * _silu(z_ref[...].astype(F32))
    sn = _rms(t, sn_ref[...]).astype(BF16)
    xn = (x_ref[...]
          + _dot(gla_ref[...], wo_ref[0:GLA_V, :])
          + _dot(diff_ref[...], wo_ref[GLA_V:GLA_V + DIFF_V, :])
          + _dot(sn, wo_ref[GLA_V + DIFF_V:MIX_WIDTH, :]))
    o_ref[...] = xn
    h = _rms(xn, g2_ref[...]).astype(BF16)
    for k in range(D_FF // MLP_TF):
        a = _dot(h, w1_ref[:, k * MLP_TF:(k + 1) * MLP_TF])
        a = jnp.square(jnp.maximum(a, 0.0)).astype(BF16)
        o_ref[...] += _dot(a, w2_ref[k * MLP_TF:(k + 1) * MLP_TF, :])


def _outmlp(x2, gla2, diff2, ssd2, p2, wo, sn, g2, w1, w2):
    t = x2.shape[0]
    tm = min(MLP_TM, t)
    row = lambda w: pl.BlockSpec((tm, w), lambda i: (i, 0))
    resident = lambda shape: pl.BlockSpec(shape, lambda i: (0, 0), pipeline_mode=pl.Buffered(1))
    return pl.pallas_call(
        _outmlp_kernel,
        grid=(t // tm,),
        in_specs=[row(D_MODEL), row(GLA_V), row(DIFF_V), row(SSD_INNER),
                  pl.BlockSpec((tm, SSD_INNER), lambda i: (i, C_SZ // SSD_INNER)),
                  resident((MIX_WIDTH, D_MODEL)),
                  pl.BlockSpec((1, SSD_INNER), lambda i: (0, 0)),
                  pl.BlockSpec((1, D_MODEL), lambda i: (0, 0)),
                  resident((D_MODEL, D_FF)),
                  resident((D_FF, D_MODEL))],
        out_specs=pl.BlockSpec((tm, D_MODEL), lambda i: (i, 0)),
        out_shape=jax.ShapeDtypeStruct((t, D_MODEL), F32),
        compiler_params=_cparams(("parallel",)),
        name="outmlp",
    )(x2, gla2, diff2, ssd2, p2, wo, sn, g2, w1, w2)


def _prep_params(norm1, w_in, gla_wg_f, gla_bg_f, gla_wg_b, gla_bg_b, gla_norm,
                 diff_qnorm, diff_knorm, diff_lq1, diff_lk1, diff_lq2, diff_lk2, diff_subln,
                 ssd_conv_w, ssd_conv_b, ssd_dt_bias_f, ssd_dt_bias_b, ssd_A_log_f, ssd_A_log_b,
                 ssd_D, ssd_norm, w_out, norm2, w_mlp1, w_mlp2):
    depth = w_in.shape[0]
    (gq, gk, gv, gg, glr, dq, dk, dv, sz, sxbc, sdt) = jnp.split(w_in, IN_OFFSETS, axis=-1)

    def group_cols(a):
        xs, bm, cm = a[..., :SSD_INNER], a[..., SSD_INNER:SSD_INNER + SSD_BC], a[..., SSD_INNER + SSD_BC:]
        parts = []
        for g in range(SSD_GROUPS):
            parts += [xs[..., g * SSD_XW:(g + 1) * SSD_XW], bm[..., g * SSD_N:(g + 1) * SSD_N],
                      cm[..., g * SSD_N:(g + 1) * SSD_N]]
        return parts

    def pad_cols(a, width):
        return jnp.pad(a, [(0, 0)] * (a.ndim - 1) + [(0, width - a.shape[-1])])

    w_p = jnp.concatenate(group_cols(sxbc) + [gq, gk, gv, gg, dq, dk, dv,
                                              pad_cols(glr, LANES), pad_cols(sdt, LANES), sz],
                          axis=-1).astype(BF16)

    npair = GLA_HEADS // GLA_HP
    wg = jnp.zeros((depth, npair, LANES, 2 * GLA_QW), F32)
    bgs = []
    for hp in range(npair):
        cs = slice(hp * GLA_QW, (hp + 1) * GLA_QW)
        wg = wg.at[:, hp, 0:GLA_RANK, 0:GLA_QW].set(gla_wg_f[:, :, cs])
        wg = wg.at[:, hp, GLA_RANK:2 * GLA_RANK, GLA_QW:].set(gla_wg_b[:, :, cs])
        bgs.append(jnp.concatenate([gla_bg_f[:, cs], gla_bg_b[:, cs]], axis=-1))
    wg = wg.astype(BF16)
    bg = jnp.stack(bgs, axis=1)[:, :, None, :]
    gn = gla_norm[:, None, :]

    qn = jnp.tile(diff_qnorm, (1, 2))[:, None, :]
    kn = jnp.tile(diff_knorm, (1, 2))[:, None, :]
    shift = (1.01 * LOG2E * DIFF_DH ** 0.5 * jnp.max(jnp.abs(diff_qnorm), axis=-1)
             * jnp.max(jnp.abs(diff_knorm), axis=-1)).astype(F32)[:, None]
    lqk = jnp.stack([diff_lq1, diff_lk1, diff_lq2, diff_lk2], axis=1)
    sub = diff_subln[:, None, :]

    cw = jnp.concatenate(group_cols(ssd_conv_w), axis=-1).reshape(depth, SSD_CONV, SSD_GROUPS, SSD_GW)
    cw = cw.transpose(0, 2, 1, 3)
    cb = jnp.concatenate(group_cols(ssd_conv_b), axis=-1).reshape(depth, SSD_GROUPS, 1, SSD_GW)
    dtb = pad_cols(jnp.concatenate([ssd_dt_bias_f, ssd_dt_bias_b], axis=-1), LANES)[:, None, :]

    e_np = np.zeros((2, SSD_GROUPS, LANES, SSD_XW), np.float32)
    for d in range(2):
        for g in range(SSD_GROUPS):
            for e in range(SSD_E):
                e_np[d, g, d * SSD_HEADS + g * SSD_E + e, e * SSD_P:(e + 1) * SSD_P] = 1.0
    e_f = jnp.asarray(e_np[0], BF16)
    e_b = jnp.asarray(e_np[1], BF16)

    def per_lane(a):
        return jnp.repeat(a.reshape(depth, SSD_GROUPS, SSD_E), SSD_P, axis=-1)[:, :, None, :]

    a_f = per_lane(-jnp.exp(ssd_A_log_f))
    a_b = per_lane(-jnp.exp(ssd_A_log_b))
    dsk = per_lane(ssd_D)

    return dict(norm1=norm1[:, None, :], w_p=w_p, wg=wg, bg=bg, gn=gn, qn=qn, kn=kn, shift=shift, lqk=lqk, sub=sub,
                cw=cw, cb=cb, dtb=dtb, e_f=e_f, e_b=e_b, a_f=a_f, a_b=a_b, dsk=dsk,
                sn=ssd_norm[:, None, :], wo=w_out.astype(BF16), norm2=norm2[:, None, :],
                w1=w_mlp1.astype(BF16), w2=w_mlp2.astype(BF16))


def _rope_tables(seq):
    inv = ROPE_THETA ** (-jnp.arange(0, ROPE_DIM, 2, dtype=F32) / ROPE_DIM)
    ang = jnp.arange(seq, dtype=F32)[:, None] * inv[None, :]
    lane = np.arange(DIFF_DV) % DIFF_DH
    idx = lane % (ROPE_DIM // 2)
    in_rope = lane < ROPE_DIM
    sign = np.where(lane < ROPE_DIM // 2, -1.0, 1.0).astype(np.float32)
    cos_t = jnp.where(in_rope[None, :], jnp.cos(ang)[:, idx], 1.0)
    sin_t = jnp.where(in_rope[None, :], jnp.sin(ang)[:, idx] * sign[None, :], 0.0)
    return cos_t.astype(F32), sin_t.astype(F32)


def _trunk(x, pr):
    bsz, seq, _ = x.shape
    cos_t, sin_t = _rope_tables(seq)
    x2 = x.reshape(bsz * seq, D_MODEL)
    for l in range(DEPTH):
        p2 = _inproj(x2, pr["norm1"][l], pr["w_p"][l])
        p3 = p2.reshape(bsz, seq, P_WIDTH)
        o_gla = _gla(p3, pr["wg"][l], pr["bg"][l], pr["gn"][l])
        lambda_init = 0.8 - 0.6 * math.exp(-0.3 * l)
        o_diff = _diff(p3, cos_t, sin_t, pr["shift"][l], pr["qn"][l], pr["kn"][l], pr["lqk"][l], pr["sub"][l],
                       lambda_init)
        o_ssd = _ssd(p3, pr["cw"][l], pr["cb"][l], pr["dtb"][l], pr["e_f"], pr["e_b"],
                     pr["a_f"][l], pr["a_b"][l], pr["dsk"][l])
        x2 = _outmlp(x2, o_gla.reshape(bsz * seq, GLA_V), o_diff.reshape(bsz * seq, DIFF_V),
                     o_ssd.reshape(bsz * seq, SSD_INNER), p2, pr["wo"][l], pr["sn"][l], pr["norm2"][l],
                     pr["w1"][l], pr["w2"][l])
    return x2.reshape(bsz, seq, D_MODEL)


def kernel(x_prompt, x_sample, norm1, w_in, gla_wg_f, gla_bg_f, gla_wg_b, gla_bg_b, gla_norm, diff_qnorm, diff_knorm, diff_lq1, diff_lk1, diff_lq2, diff_lk2, diff_subln, ssd_conv_w, ssd_conv_b, ssd_dt_bias_f, ssd_dt_bias_b, ssd_A_log_f, ssd_A_log_b, ssd_D, ssd_norm, w_out, norm2, w_mlp1, w_mlp2):
    pr = _prep_params(norm1, w_in, gla_wg_f, gla_bg_f, gla_wg_b, gla_bg_b, gla_norm,
                      diff_qnorm, diff_knorm, diff_lq1, diff_lk1, diff_lq2, diff_lk2, diff_subln,
                      ssd_conv_w, ssd_conv_b, ssd_dt_bias_f, ssd_dt_bias_b, ssd_A_log_f, ssd_A_log_b,
                      ssd_D, ssd_norm, w_out, norm2, w_mlp1, w_mlp2)
    return (_trunk(x_prompt, pr), _trunk(x_sample, pr))
```

```python
import functools
import math

import numpy as np
import jax
import jax.numpy as jnp
from jax import lax
from jax.experimental import pallas as pl
from jax.experimental.pallas import tpu as pltpu

F32 = jnp.float32
BF16 = jnp.bfloat16

D_MODEL = 1024
DEPTH = 4
GLA_HEADS = 4
GLA_DK = 64
GLA_DV = 128
GLA_RANK = 16
GLA_GATE_NORM = 16.0
DIFF_HEADS = 4
DIFF_DH = 64
DIFF_DV = 2 * DIFF_DH
ROPE_THETA = 500000.0
ROPE_DIM = DIFF_DH // 4
SSD_HEADS = 8
SSD_P = 64
SSD_GROUPS = 2
SSD_E = SSD_HEADS // SSD_GROUPS
SSD_N = 64
SSD_CONV = 5
CHUNK = 64
D_FF = 4 * D_MODEL
EPS = 1e-6

GLA_QK = GLA_HEADS * GLA_DK
GLA_V = GLA_HEADS * GLA_DV
DIFF_QK = DIFF_HEADS * 2 * DIFF_DH
DIFF_V = DIFF_HEADS * DIFF_DV
SSD_INNER = SSD_HEADS * SSD_P
SSD_BC = SSD_GROUPS * SSD_N
SSD_XBC = SSD_INNER + 2 * SSD_BC
MIX_WIDTH = GLA_V + DIFF_V + SSD_INNER
IN_SIZES = (GLA_QK, GLA_QK, GLA_V, GLA_V, 2 * GLA_RANK,
            DIFF_QK, DIFF_QK, DIFF_V,
            SSD_INNER, SSD_XBC, 2 * SSD_HEADS)
IN_OFFSETS = tuple(int(v) for v in np.cumsum(IN_SIZES)[:-1])

LANES = 128

SSD_GW = SSD_E * SSD_P + 2 * SSD_N
C_SX = 0
C_GQ = C_SX + SSD_GROUPS * SSD_GW
C_GK = C_GQ + GLA_QK
C_GV = C_GK + GLA_QK
C_GG = C_GV + GLA_V
C_DQ = C_GG + GLA_V
C_DK = C_DQ + DIFF_QK
C_DV = C_DK + DIFF_QK
C_GLR = C_DV + DIFF_V
C_SDT = C_GLR + LANES
C_SZ = C_SDT + LANES
P_WIDTH = C_SZ + SSD_INNER

VMEM_LIMIT = 56 * 1024 * 1024
SCAN_UNROLL = 2
GROUP = 4
GRP_ROWS = GROUP * CHUNK
PAR_GROUPS = 4


def _cparams(sem):
    return pltpu.CompilerParams(dimension_semantics=sem, vmem_limit_bytes=VMEM_LIMIT)


def _dot(a, b):
    return jnp.dot(a, b, preferred_element_type=F32)


def _dot_nt(a, b):
    return lax.dot_general(a, b, (((1,), (1,)), ((), ())), preferred_element_type=F32)


def _dot_tn(a, b):
    return lax.dot_general(a, b, (((0,), (0,)), ((), ())), preferred_element_type=F32)


def _split(x):
    hi = x.astype(BF16)
    lo = (x - hi.astype(F32)).astype(BF16)
    return hi, lo


def _dot_exact_rhs(a01, x):
    hi, lo = _split(x)
    return _dot(a01, hi) + _dot(a01, lo)


def _dot_exact_lhs(x, b01):
    hi, lo = _split(x)
    return _dot(hi, b01) + _dot(lo, b01)


def _dot_select(sel01, x):
    hi = x.astype(BF16)
    r1 = x - hi.astype(F32)
    mid = r1.astype(BF16)
    lo = (r1 - mid.astype(F32)).astype(BF16)
    return _dot(sel01, hi) + _dot(sel01, mid) + _dot(sel01, lo)


def _silu(x):
    return x * (1.0 / (1.0 + jnp.exp(-x)))


def _softplus(x):
    return jnp.maximum(x, 0.0) + jnp.log(1.0 + jnp.exp(-jnp.abs(x)))


def _rms(x, g):
    ms = jnp.mean(x * x, axis=-1, keepdims=True)
    return x * lax.rsqrt(ms + EPS) * g


IN_TM = 512
IN_TN = 512


def _inproj_kernel(x_ref, g_ref, w_ref, o_ref):
    h = _rms(x_ref[...], g_ref[...]).astype(BF16)
    for j in range(P_WIDTH // IN_TN):
        o_ref[:, j * IN_TN:(j + 1) * IN_TN] = _dot(h, w_ref[:, j * IN_TN:(j + 1) * IN_TN]).astype(BF16)


def _inproj(x2, g, w):
    t = x2.shape[0]
    tm = min(IN_TM, t)
    return pl.pallas_call(
        _inproj_kernel,
        grid=(t // tm,),
        in_specs=[pl.BlockSpec((tm, D_MODEL), lambda i: (i, 0)),
                  pl.BlockSpec((1, D_MODEL), lambda i: (0, 0)),
                  pl.BlockSpec((D_MODEL, P_WIDTH), lambda i: (0, 0))],
        out_specs=pl.BlockSpec((tm, P_WIDTH), lambda i: (i, 0)),
        out_shape=jax.ShapeDtypeStruct((t, P_WIDTH), BF16),
        compiler_params=_cparams(("parallel",)),
        name="inproj",
    )(x2, g, w)


GLA_HP = 2
GLA_MAX_LOGDECAY = 60.0
GLA_QW = GLA_HP * GLA_DK
GLA_VW = GLA_HP * GLA_DV


def _gla_kernel(q_ref, k_ref, v_ref, gg_ref, glr_ref, wg_ref, bg_ref, gn_ref, o_ref,
                upd_ref, qe_ref, oacc_ref, d_ref, st_ref, *, seq):
    nc = seq // CHUNK
    ng = seq // GRP_ROWS
    scale = GLA_DK ** -0.5

    row = lax.broadcasted_iota(jnp.int32, (GRP_ROWS, GRP_ROWS), 0)
    col = lax.broadcasted_iota(jnp.int32, (GRP_ROWS, GRP_ROWS), 1)
    tri = ((row // CHUNK == col // CHUNK) & (col <= row)).astype(BF16)
    row2 = lax.broadcasted_iota(jnp.int32, (GLA_HP * GRP_ROWS, GRP_ROWS), 0) % GRP_ROWS
    col2 = lax.broadcasted_iota(jnp.int32, (GLA_HP * GRP_ROWS, GRP_ROWS), 1)
    same_chunk = row2 // CHUNK == col2 // CHUNK
    masks = (same_chunk & (col2 <= row2), same_chunk & (col2 >= row2))
    head0 = lax.broadcasted_iota(jnp.int32, (GRP_ROWS, GLA_QW), 1) < GLA_DK
    is_fwd = lax.broadcasted_iota(jnp.int32, (GRP_ROWS, 2 * GLA_QW), 1) < GLA_QW
    st_r = lax.broadcasted_iota(jnp.int32, (GLA_VW, GLA_QW), 0) // GLA_DV
    st_c = lax.broadcasted_iota(jnp.int32, (GLA_VW, GLA_QW), 1) // GLA_DK
    same_head = st_r == st_c

    n_par = PAR_GROUPS if ng % PAR_GROUPS == 0 else 1
    dirs = (slice(0, GLA_QW), slice(GLA_QW, 2 * GLA_QW))

    def gates(r):
        x = _dot(glr_ref[pl.ds(r, GRP_ROWS), :], wg_ref[...]) + bg_ref[...]
        return -_softplus(-x) * (1.0 / GLA_GATE_NORM)

    def decayed(r, lg, pre):
        tot = jnp.concatenate(
            [jnp.broadcast_to(pre[g * CHUNK + CHUNK - 1:(g + 1) * CHUNK, :], (CHUNK, 2 * GLA_QW))
             for g in range(GROUP)], axis=0)
        b = jnp.where(is_fwd, pre, tot - pre + lg)
        q = q_ref[pl.ds(r, GRP_ROWS), :].astype(F32)
        k = k_ref[pl.ds(r, GRP_ROWS), :].astype(F32)
        qe = jnp.concatenate([q, q], axis=1) * jnp.exp(b) * scale
        kk = jnp.concatenate([k, k], axis=1)
        ke = (kk * jnp.exp(-b)).astype(BF16)
        kl = (kk * jnp.exp(tot - b)).astype(BF16)
        q2 = [by_head(qe[:, sl]) for sl in dirs]
        return tot, qe.astype(BF16), ke, kl, q2, (b, q, k)

    def by_head(x):
        return jnp.concatenate([jnp.where(head0, x, 0.0), jnp.where(head0, 0.0, x)], axis=0).astype(BF16)

    def steep_scores(d, b, q, k):
        pi = row2 % CHUNK
        pj = col2 % CHUNK
        total = jnp.where(same_chunk & (pi == pj), _dot_nt(by_head(q * scale), k.astype(BF16)), 0.0)
        srow = row % CHUNK
        s = CHUNK // 2
        while s >= 1:
            blk_i, blk_j = pi // s, pj // s
            sib = same_chunk & (blk_i // 2 == blk_j // 2)
            if d == 0:
                pair = sib & (blk_i % 2 == 1) & (blk_j % 2 == 0)
                sel_q = (srow // s >= 1) & (col == row - srow % s - 1)
                sel_k = col == row - srow % s + s - 1
            else:
                pair = sib & (blk_i % 2 == 0) & (blk_j % 2 == 1)
                sel_q = (srow // s < CHUNK // s - 1) & (col == row - srow % s + s)
                sel_k = col == row - srow % s
            ref_q = _dot_select(sel_q.astype(BF16), b)
            ref_k = _dot_select(sel_k.astype(BF16), b)
            qt = q * jnp.exp(b - ref_q) * scale
            kt = (k * jnp.exp(ref_k - b)).astype(BF16)
            total = total + jnp.where(pair, _dot_nt(by_head(qt), kt), 0.0)
            s //= 2
        return total

    def phase_a(i, carry):
        ts = [i + j * (ng // n_par) for j in range(n_par)]
        rs = [pl.multiple_of(t * GRP_ROWS, GRP_ROWS) for t in ts]
        lgs = [gates(r) for r in rs]
        steep = functools.reduce(jnp.maximum, [jnp.max(-lg) for lg in lgs]) * CHUNK > GLA_MAX_LOGDECAY

        @pl.when(steep)
        def _():
            for t, r, lg in zip(ts, rs, lgs):
                phase_a_rest([t], [r], [lg], True)

        @pl.when(jnp.logical_not(steep))
        def _():
            phase_a_rest(ts, rs, lgs, False)

        return carry

    def phase_a_rest(ts, rs, lgs, steep):
        pres = [_dot_exact_rhs(tri, lg) for lg in lgs]
        dec = [decayed(r, lg, pre) for r, lg, pre in zip(rs, lgs, pres)]
        vs = [v_ref[pl.ds(r, GRP_ROWS), :] for r in rs]
        if steep:
            scores = [[steep_scores(d, b[:, dirs[d]], q, k) for d in range(2)] for (*_, (b, q, k)) in dec]
        else:
            scores = [[_dot_nt(q2[d], ke[:, dirs[d]]) for d in range(2)] for (_, _, ke, _, q2, _) in dec]
        upds = [[[_dot_tn(v[g * CHUNK:(g + 1) * CHUNK], kl[g * CHUNK:(g + 1) * CHUNK, dirs[d]])
                  for g in range(GROUP)] for d in range(2)] for v, (_, _, _, kl, _, _) in zip(vs, dec)]
        for t, r, v, (tot, qe, _, _, _, _), sc, up in zip(ts, rs, vs, dec, scores, upds):
            qe_ref[pl.ds(r, GRP_ROWS), :] = qe
            o_sum = None
            for d in range(2):
                a = jnp.where(masks[d], sc[d], 0.0).astype(BF16)
                o_d = jnp.concatenate(
                    [_dot(a[h * GRP_ROWS:(h + 1) * GRP_ROWS], v[:, h * GLA_DV:(h + 1) * GLA_DV])
                     for h in range(GLA_HP)], axis=1)
                o_sum = o_d if o_sum is None else o_sum + o_d
                for g in range(GROUP):
                    upd_ref[d * nc + t * GROUP + g] = jnp.where(same_head, up[d][g], 0.0).astype(BF16)
            oacc_ref[pl.ds(r, GRP_ROWS), :] = o_sum
            for g in range(GROUP):
                d_ref[t * GROUP + g] = jnp.exp(tot[g * CHUNK:g * CHUNK + 8, :])

    lax.fori_loop(0, ng // n_par, phase_a, 0)

    st_ref[...] = jnp.zeros_like(st_ref)

    def phase_b(c, carry):
        cb = nc - 1 - c
        for d, idx, cd in ((0, c, c), (1, nc + cb, cb)):
            dec = d_ref[cd][0:1, d * GLA_QW:(d + 1) * GLA_QW]
            u = upd_ref[idx].astype(F32)
            s = st_ref[d]
            upd_ref[idx] = s.astype(BF16)
            st_ref[d] = s * dec + u
        return carry

    lax.fori_loop(0, nc, phase_b, 0, unroll=SCAN_UNROLL)

    def phase_c(i, carry):
        ts = [i + j * (ng // n_par) for j in range(n_par)]
        rs = [pl.multiple_of(t * GRP_ROWS, GRP_ROWS) for t in ts]
        inter = []
        for t, r in zip(ts, rs):
            parts = []
            for g in range(GROUP):
                qe = qe_ref[pl.ds(r + g * CHUNK, CHUNK), :]
                ods = [_dot_nt(qe[:, dirs[d]], upd_ref[d * nc + t * GROUP + g].astype(BF16)) for d in range(2)]
                parts.append(ods[0] + ods[1])
            inter.append(parts)
        for r, parts in zip(rs, inter):
            o = oacc_ref[pl.ds(r, GRP_ROWS), :] + jnp.concatenate(parts, axis=0)
            gg = gg_ref[pl.ds(r, GRP_ROWS), :].astype(F32)
            normed = [_rms(o[:, h * GLA_DV:(h + 1) * GLA_DV], gn_ref[...]) for h in range(GLA_HP)]
            y = jnp.concatenate(normed, axis=1) * _silu(gg)
            o_ref[pl.ds(r, GRP_ROWS), :] = y.astype(o_ref.dtype)
        return carry

    lax.fori_loop(0, ng // n_par, phase_c, 0)


def _gla(p3, wg, bg, gn):
    bsz, seq, _ = p3.shape
    assert seq % GRP_ROWS == 0
    nc = seq // CHUNK
    npair = GLA_HEADS // GLA_HP
    return pl.pallas_call(
        functools.partial(_gla_kernel, seq=seq),
        grid=(bsz, npair),
        in_specs=[pl.BlockSpec((None, seq, GLA_QW), lambda b, h: (b, 0, C_GQ // GLA_QW + h)),
                  pl.BlockSpec((None, seq, GLA_QW), lambda b, h: (b, 0, C_GK // GLA_QW + h)),
                  pl.BlockSpec((None, seq, GLA_VW), lambda b, h: (b, 0, C_GV // GLA_VW + h)),
                  pl.BlockSpec((None, seq, GLA_VW), lambda b, h: (b, 0, C_GG // GLA_VW + h)),
                  pl.BlockSpec((None, seq, LANES), lambda b, h: (b, 0, C_GLR // LANES)),
                  pl.BlockSpec((None, LANES, 2 * GLA_QW), lambda b, h: (h, 0, 0)),
                  pl.BlockSpec((None, 1, 2 * GLA_QW), lambda b, h: (h, 0, 0)),
                  pl.BlockSpec((1, GLA_DV), lambda b, h: (0, 0))],
        out_specs=pl.BlockSpec((None, seq, GLA_VW), lambda b, h: (b, 0, h)),
        out_shape=jax.ShapeDtypeStruct((bsz, seq, GLA_V), BF16),
        scratch_shapes=[pltpu.VMEM((2 * nc, GLA_VW, GLA_QW), BF16),
                        pltpu.VMEM((seq, 2 * GLA_QW), BF16),
                        pltpu.VMEM((seq, GLA_VW), F32),
                        pltpu.VMEM((nc, 8, 2 * GLA_QW), F32),
                        pltpu.VMEM((2, GLA_VW, GLA_QW), F32)],
        compiler_params=_cparams(("parallel", "parallel")),
        name="gla",
    )(p3, p3, p3, p3, p3, wg, bg, gn)


DIFF_QB = 512
DIFF_NSUB = 4
DIFF_TK = 256
DIFF_PREP_ROWS = 256
DIFF_PREP_TILES = 2
LOG2E = math.log2(math.e)
DIFF_MAX_SHIFT = 60.0


def _diff_kernel(shift_ref, q_ref, k_ref, v_ref, cos_ref, sin_ref, qn_ref, kn_ref, lqk_ref, sub_ref, o_ref,
                 qp_ref, kp_ref, *, seq, lambda_init):
    qi = pl.program_id(2)
    lane = lax.broadcasted_iota(jnp.int32, (1, DIFF_DV), 1)
    comp0 = lane < DIFF_DH

    @pl.when(qi == 0)
    def _():
        rows = min(DIFF_PREP_ROWS, seq)
        src = lax.broadcasted_iota(jnp.int32, (DIFF_DV, DIFF_DV), 0)
        dst = lax.broadcasted_iota(jnp.int32, (DIFF_DV, DIFF_DV), 1)
        same_comp = (src // DIFF_DH == dst // DIFF_DH).astype(BF16)
        half = ROPE_DIM // 2
        dpos = dst % DIFF_DH
        partner_of = ((dpos < half) & (src == dst + half)) | ((dpos >= half) & (dpos < ROPE_DIM) & (src == dst - half))
        partner_of = partner_of.astype(BF16)

        tiles = DIFF_PREP_TILES if (seq // rows) % DIFF_PREP_TILES == 0 else 1
        work = ((q_ref, qn_ref, qp_ref, DIFF_DH ** -0.5 * LOG2E), (k_ref, kn_ref, kp_ref, 1.0))

        def body(i, carry):
            rs = [pl.multiple_of((i * tiles + j) * rows, rows) for j in range(tiles)]
            items = [(src_ref, g_ref, dst_ref, mul, r) for r in rs for (src_ref, g_ref, dst_ref, mul) in work]
            xs = [src_ref[pl.ds(r, rows), :].astype(F32) for (src_ref, _, _, _, r) in items]
            ms = [_dot((x * x).astype(BF16), same_comp) * (1.0 / DIFF_DH) for x in xs]
            xn = [x * lax.rsqrt(m + EPS) * it[1][...] for x, m, it in zip(xs, ms, items)]
            partner = [_dot(v.astype(BF16), partner_of) for v in xn]
            for v, p, (_, _, dst_ref, mul, r) in zip(xn, partner, items):
                out = v * cos_ref[pl.ds(r, rows), :] + p * sin_ref[pl.ds(r, rows), :]
                dst_ref[pl.ds(r, rows), :] = (out * mul).astype(BF16)
            return carry

        lax.fori_loop(0, seq // (rows * tiles), body, 0)

    lqk = lqk_ref[...]
    l1 = jnp.sum(lqk[0:1] * lqk[1:2], axis=-1, keepdims=True)
    l2 = jnp.sum(lqk[2:3] * lqk[3:4], axis=-1, keepdims=True)
    lam = jnp.exp(l1) - jnp.exp(l2) + lambda_init

    def stacked_q(r0, n):
        qb = qp_ref[pl.ds(r0, n), :]
        zero = jnp.zeros_like(qb)
        return jnp.concatenate([jnp.where(comp0, qb, zero), jnp.where(comp0, zero, qb)], axis=0)

    def finish(rows, n, ov, l):
        rinv = 1.0 / l
        o = ov[0:n] * rinv[0:n] - ov[n:] * (lam * rinv[n:])
        o = _rms(o, sub_ref[...]) * (1.0 - lambda_init)
        o_ref[rows, :] = o.astype(o_ref.dtype)

    shift = shift_ref[0]

    @pl.when(shift <= DIFF_MAX_SHIFT)
    def _():
        qz = stacked_q(pl.multiple_of(qi * DIFF_QB, DIFF_QB), DIFF_QB)
        nkt = seq // DIFF_TK
        lacc = jnp.zeros((2 * DIFF_QB, LANES), F32)
        ov = jnp.zeros((2 * DIFF_QB, DIFF_DV), F32)
        s_prev = None
        for j in range(nkt + 1):
            s_cur = _dot_nt(qz, kp_ref[j * DIFF_TK:(j + 1) * DIFF_TK, :]) if j < nkt else None
            if s_prev is not None:
                e = jnp.exp2(s_prev - shift)
                for c in range(DIFF_TK // LANES):
                    lacc = lacc + e[:, c * LANES:(c + 1) * LANES]
                ov = ov + _dot(e.astype(BF16), v_ref[(j - 1) * DIFF_TK:j * DIFF_TK, :])
            s_prev = s_cur
        finish(slice(0, DIFF_QB), DIFF_QB, ov, jnp.sum(lacc, axis=-1, keepdims=True))

    @pl.when(shift > DIFF_MAX_SHIFT)
    def _():
        sb = DIFF_QB // DIFF_NSUB
        scores = [_dot_nt(stacked_q(pl.multiple_of(qi * DIFF_QB + j * sb, sb), sb), kp_ref[...])
                  for j in range(DIFF_NSUB)]
        for j, s in enumerate(scores):
            e = jnp.exp2(s - jnp.max(s, axis=-1, keepdims=True))
            finish(slice(j * sb, (j + 1) * sb), sb, _dot(e.astype(BF16), v_ref[...]),
                   jnp.sum(e, axis=-1, keepdims=True))


def _diff(p3, cos_t, sin_t, shift, qn, kn, lqk, sub, lambda_init):
    bsz, seq, _ = p3.shape
    qb = DIFF_QB
    assert seq % qb == 0 and seq % DIFF_TK == 0
    return pl.pallas_call(
        functools.partial(_diff_kernel, seq=seq, lambda_init=lambda_init),
        grid=(bsz, DIFF_HEADS, seq // qb),
        in_specs=[pl.BlockSpec(memory_space=pltpu.SMEM),
                  pl.BlockSpec((None, seq, DIFF_DV), lambda b, h, i: (b, 0, C_DQ // DIFF_DV + h)),
                  pl.BlockSpec((None, seq, DIFF_DV), lambda b, h, i: (b, 0, C_DK // DIFF_DV + h)),
                  pl.BlockSpec((None, seq, DIFF_DV), lambda b, h, i: (b, 0, C_DV // DIFF_DV + h)),
                  pl.BlockSpec((seq, DIFF_DV), lambda b, h, i: (0, 0)),
                  pl.BlockSpec((seq, DIFF_DV), lambda b, h, i: (0, 0)),
                  pl.BlockSpec((1, DIFF_DV), lambda b, h, i: (0, 0)),
                  pl.BlockSpec((1, DIFF_DV), lambda b, h, i: (0, 0)),
                  pl.BlockSpec((4, DIFF_DH), lambda b, h, i: (0, 0)),
                  pl.BlockSpec((1, DIFF_DV), lambda b, h, i: (0, 0))],
        out_specs=pl.BlockSpec((None, qb, DIFF_DV), lambda b, h, i: (b, i, h)),
        out_shape=jax.ShapeDtypeStruct((bsz, seq, DIFF_V), BF16),
        scratch_shapes=[pltpu.VMEM((seq, DIFF_DV), BF16),
                        pltpu.VMEM((seq, DIFF_DV), BF16)],
        compiler_params=_cparams(("parallel", "parallel", "arbitrary")),
        name="diffattn",
    )(shift, p3, p3, p3, cos_t, sin_t, qn, kn, lqk, sub)


SSD_XW = SSD_E * SSD_P
SSD_HALO = 8
SSD_CONV_ROWS = 256
SSD_PAR_GROUPS = 2


def _ssd_kernel(xbc_ref, sdt_ref, cw_ref, cb_ref, dtb_ref, ef_ref, eb_ref, af_ref, ab_ref, dsk_ref, o_ref,
                xpad_ref, xc_ref, dt_ref, yacc_ref, upd_ref, eac_ref, d_ref, st_ref, *, seq):
    nc = seq // CHUNK
    ng = seq // GRP_ROWS
    rows = min(SSD_CONV_ROWS, seq)
    pad = SSD_CONV // 2

    xpad_ref[0:SSD_HALO, :] = jnp.zeros((SSD_HALO, SSD_GW), F32)
    xpad_ref[seq + SSD_HALO:seq + 2 * SSD_HALO, :] = jnp.zeros((SSD_HALO, SSD_GW), F32)

    def copy_body(i, carry):
        r = pl.multiple_of(i * rows, rows)
        xpad_ref[pl.ds(r + SSD_HALO, rows), :] = xbc_ref[pl.ds(r, rows), :].astype(F32)
        dt_ref[pl.ds(r, rows), :] = _softplus(sdt_ref[pl.ds(r, rows), :].astype(F32) + dtb_ref[...])
        return carry

    lax.fori_loop(0, seq // rows, copy_body, 0)

    def conv_body(i, carry):
        r = pl.multiple_of(i * rows, rows)
        win = xpad_ref[pl.ds(r, rows + 2 * SSD_HALO), :]
        acc = cb_ref[...] + win[SSD_HALO - pad:SSD_HALO - pad + rows] * cw_ref[0:1, :]
        for w in range(1, SSD_CONV):
            acc = acc + win[SSD_HALO - pad + w:SSD_HALO - pad + w + rows] * cw_ref[w:w + 1, :]
        xc_ref[pl.ds(r, rows), :] = _silu(acc)
        return carry

    lax.fori_loop(0, seq // rows, conv_body, 0)
    row = lax.broadcasted_iota(jnp.int32, (GRP_ROWS, GRP_ROWS), 0)
    col = lax.broadcasted_iota(jnp.int32, (GRP_ROWS, GRP_ROWS), 1)
    same_chunk = row // CHUNK == col // CHUNK
    tri = (same_chunk & (col <= row)).astype(BF16)
    blk = same_chunk.astype(BF16)
    rowc = lax.broadcasted_iota(jnp.int32, (GRP_ROWS, SSD_XW), 0) % CHUNK
    colj = lax.broadcasted_iota(jnp.int32, (GRP_ROWS, SSD_XW), 1) % CHUNK
    le = colj <= rowc
    ge = colj >= rowc
    mask_ij = (le, ge)
    mask_sum = (ge, le)
    bd_r = lax.broadcasted_iota(jnp.int32, (SSD_E * CHUNK, SSD_XW), 0) // CHUNK
    bd_c = lax.broadcasted_iota(jnp.int32, (SSD_E * CHUNK, SSD_XW), 1) // SSD_P
    same_head = bd_r == bd_c
    e_refs = (ef_ref, eb_ref)
    a_refs = (af_ref, ab_ref)
    dirs = (slice(0, SSD_XW), slice(SSD_XW, 2 * SSD_XW))
    n_par = SSD_PAR_GROUPS if ng % SSD_PAR_GROUPS == 0 else 1

    def chunk_rows(g):
        return slice(g * CHUNK, (g + 1) * CHUNK)

    def decays(r, d):
        dt4 = _dot_exact_lhs(dt_ref[pl.ds(r, GRP_ROWS), :], e_refs[d][...])
        la4 = dt4 * a_refs[d][...]
        pre = _dot_exact_rhs(tri, la4)
        asum = _dot_exact_rhs(blk, jnp.where(mask_sum[d], la4, 0.0))
        return dt4, la4, pre, asum

    def phase_a(i, carry):
        ts = [i + j * (ng // n_par) for j in range(n_par)]
        rs = [pl.multiple_of(t * GRP_ROWS, GRP_ROWS) for t in ts]
        dec = [[decays(r, d) for d in range(2)] for r in rs]
        xs = [xc_ref[pl.ds(r, GRP_ROWS), 0:SSD_XW] for r in rs]
        bms = [xc_ref[pl.ds(r, GRP_ROWS), SSD_XW:SSD_XW + SSD_N].astype(BF16) for r in rs]
        cms = [xc_ref[pl.ds(r, GRP_ROWS), SSD_XW + SSD_N:SSD_XW + 2 * SSD_N].astype(BF16) for r in rs]
        cbs = [jnp.concatenate([_dot_nt(cm[chunk_rows(g)], jnp.concatenate([bm[chunk_rows(g)]] * SSD_E, axis=0))
                                for g in range(GROUP)], axis=0) for bm, cm in zip(bms, cms)]
        stage = []
        for x, cb4, dd in zip(xs, cbs, dec):
            per_dir = []
            for d in range(2):
                dt4, la4, pre, asum = dd[d]
                tot = jnp.concatenate(
                    [jnp.broadcast_to(pre[g * CHUNK + CHUNK - 1:(g + 1) * CHUNK, :], (CHUNK, SSD_XW))
                     for g in range(GROUP)], axis=0)
                acol = pre if d == 0 else tot - pre + la4
                lm = jnp.exp(jnp.where(mask_ij[d], acol - asum, -jnp.inf))
                m4 = (cb4 * lm).astype(BF16)
                xd = x * dt4
                wx = (xd * jnp.exp(tot - acol)).astype(BF16)
                per_dir.append((tot, acol, m4, xd, wx))
            stage.append(per_dir)
        for t, r, x, bm, per_dir in zip(ts, rs, xs, bms, stage):
            y = dsk_ref[...] * x
            for d in range(2):
                tot, acol, m4, xd, wx = per_dir[d]
                parts = []
                for g in range(GROUP):
                    cr = chunk_rows(g)
                    xbd = jnp.where(same_head, jnp.concatenate([xd[cr]] * SSD_E, axis=0), 0.0).astype(BF16)
                    parts.append(_dot(m4[cr], xbd))
                    upd_ref[d * nc + t * GROUP + g] = _dot_tn(bm[cr], wx[cr]).astype(BF16)
                y = y + jnp.concatenate(parts, axis=0)
                eac_ref[pl.ds(r, GRP_ROWS), dirs[d]] = jnp.exp(acol).astype(BF16)
                for g in range(GROUP):
                    d_ref[t * GROUP + g, :, dirs[d]] = jnp.exp(tot[g * CHUNK:g * CHUNK + 8, :])
            yacc_ref[pl.ds(r, GRP_ROWS), :] = y
        return carry

    lax.fori_loop(0, ng // n_par, phase_a, 0)

    st_ref[...] = jnp.zeros_like(st_ref)

    def phase_b(c, carry):
        cb = nc - 1 - c
        for d, idx, cd in ((0, c, c), (1, nc + cb, cb)):
            decay = d_ref[cd][0:1, dirs[d]]
            u = upd_ref[idx].astype(F32)
            s = st_ref[d]
            upd_ref[idx] = s.astype(BF16)
            st_ref[d] = s * decay + u
        return carry

    lax.fori_loop(0, nc, phase_b, 0, unroll=SCAN_UNROLL)

    def phase_c(i, carry):
        ts = [i + j * (ng // n_par) for j in range(n_par)]
        rs = [pl.multiple_of(t * GRP_ROWS, GRP_ROWS) for t in ts]
        inter = []
        for t, r in zip(ts, rs):
            cm = xc_ref[pl.ds(r, GRP_ROWS), SSD_XW + SSD_N:SSD_XW + 2 * SSD_N].astype(BF16)
            per_dir = []
            for d in range(2):
                per_dir.append(jnp.concatenate(
                    [_dot(cm[chunk_rows(g)], upd_ref[d * nc + t * GROUP + g].astype(BF16)) for g in range(GROUP)],
                    axis=0))
            inter.append(per_dir)
        for r, per_dir in zip(rs, inter):
            y = yacc_ref[pl.ds(r, GRP_ROWS), :]
            for d in range(2):
                y = y + per_dir[d] * eac_ref[pl.ds(r, GRP_ROWS), dirs[d]].astype(F32)
            o_ref[pl.ds(r, GRP_ROWS), :] = y.astype(o_ref.dtype)
        return carry

    lax.fori_loop(0, ng // n_par, phase_c, 0)


def _ssd(p3, cw, cb, dtb, ef, eb, af, ab, dsk):
    bsz, seq, _ = p3.shape
    assert seq % GRP_ROWS == 0
    nc = seq // CHUNK
    g3 = lambda shape: pl.BlockSpec((None,) + shape, lambda b, g: (g, 0, 0))
    return pl.pallas_call(
        functools.partial(_ssd_kernel, seq=seq),
        grid=(bsz, SSD_GROUPS),
        in_specs=[pl.BlockSpec((None, seq, SSD_GW), lambda b, g: (b, 0, C_SX // SSD_GW + g)),
                  pl.BlockSpec((None, seq, LANES), lambda b, g: (b, 0, C_SDT // LANES)),
                  g3((SSD_CONV, SSD_GW)), g3((1, SSD_GW)),
                  pl.BlockSpec((1, LANES), lambda b, g: (0, 0)),
                  g3((LANES, SSD_XW)), g3((LANES, SSD_XW)),
                  g3((1, SSD_XW)), g3((1, SSD_XW)), g3((1, SSD_XW))],
        out_specs=pl.BlockSpec((None, seq, SSD_XW), lambda b, g: (b, 0, g)),
        out_shape=jax.ShapeDtypeStruct((bsz, seq, SSD_INNER), BF16),
        scratch_shapes=[pltpu.VMEM((seq + 2 * SSD_HALO, SSD_GW), F32),
                        pltpu.VMEM((seq, SSD_GW), F32),
                        pltpu.VMEM((seq, LANES), F32),
                        pltpu.VMEM((seq, SSD_XW), F32),
                        pltpu.VMEM((2 * nc, SSD_N, SSD_XW), BF16),
                        pltpu.VMEM((seq, 2 * SSD_XW), BF16),
                        pltpu.VMEM((nc, 8, 2 * SSD_XW), F32),
                        pltpu.VMEM((2, SSD_N, SSD_XW), F32)],
        compiler_params=_cparams(("parallel", "parallel")),
        name="ssd",
    )(p3, p3, cw, cb, dtb, ef, eb, af, ab, dsk)


MLP_TM = 512
MLP_TF = 1024


def _outmlp_kernel(x_ref, gla_ref, diff_ref, ssd_ref, z_ref, wo_ref, sn_ref, g2_ref, w1_ref, w2_ref, o_ref):
    t = ssd_ref[...].astype(F32) * _silu(z_ref[...].astype(F32))
    sn = _rms(t, sn_ref[...]).astype(BF16)
    xn = (x_ref[...]
          + _dot(gla_ref[...], wo_ref[0:GLA_V, :])
          + _dot(diff_ref[...], wo_ref[GLA_V:GLA_V + DIFF_V, :])
          + _dot(sn, wo_ref[GLA_V + DIFF_V:MIX_WIDTH, :]))
    o_ref[...] = xn
    h = _rms(xn, g2_ref[...]).astype(BF16)
    for k in range(D_FF // MLP_TF):
        a = _dot(h, w1_ref[:, k * MLP_TF:(k + 1) * MLP_TF])
        a = jnp.square(jnp.maximum(a, 0.0)).astype(BF16)
        o_ref[...] += _dot(a, w2_ref[k * MLP_TF:(k + 1) * MLP_TF, :])


def _outmlp(x2, gla2, diff2, ssd2, p2, wo, sn, g2, w1, w2):
    t = x2.shape[0]
    tm = min(MLP_TM, t)
    row = lambda w: pl.BlockSpec((tm, w), lambda i: (i, 0))
    resident = lambda shape: pl.BlockSpec(shape, lambda i: (0, 0), pipeline_mode=pl.Buffered(1))
    return pl.pallas_call(
        _outmlp_kernel,
        grid=(t // tm,),
        in_specs=[row(D_MODEL), row(GLA_V), row(DIFF_V), row(SSD_INNER),
                  pl.BlockSpec((tm, SSD_INNER), lambda i: (i, C_SZ // SSD_INNER)),
                  resident((MIX_WIDTH, D_MODEL)),
                  pl.BlockSpec((1, SSD_INNER), lambda i: (0, 0)),
                  pl.BlockSpec((1, D_MODEL), lambda i: (0, 0)),
                  resident((D_MODEL, D_FF)),
                  resident((D_FF, D_MODEL))],
        out_specs=pl.BlockSpec((tm, D_MODEL), lambda i: (i, 0)),
        out_shape=jax.ShapeDtypeStruct((t, D_MODEL), F32),
        compiler_params=_cparams(("parallel",)),
        name="outmlp",
    )(x2, gla2, diff2, ssd2, p2, wo, sn, g2, w1, w2)


def _prep_params(norm1, w_in, gla_wg_f, gla_bg_f, gla_wg_b, gla_bg_b, gla_norm,
                 diff_qnorm, diff_knorm, diff_lq1, diff_lk1, diff_lq2, diff_lk2, diff_subln,
                 ssd_conv_w, ssd_conv_b, ssd_dt_bias_f, ssd_dt_bias_b, ssd_A_log_f, ssd_A_log_b,
                 ssd_D, ssd_norm, w_out, norm2, w_mlp1, w_mlp2):
    depth = w_in.shape[0]
    (gq, gk, gv, gg, glr, dq, dk, dv, sz, sxbc, sdt) = jnp.split(w_in, IN_OFFSETS, axis=-1)

    def group_cols(a):
        xs, bm, cm = a[..., :SSD_INNER], a[..., SSD_INNER:SSD_INNER + SSD_BC], a[..., SSD_INNER + SSD_BC:]
        parts = []
        for g in range(SSD_GROUPS):
            parts += [xs[..., g * SSD_XW:(g + 1) * SSD_XW], bm[..., g * SSD_N:(g + 1) * SSD_N],
                      cm[..., g * SSD_N:(g + 1) * SSD_N]]
        return parts

    def pad_cols(a, width):
        return jnp.pad(a, [(0, 0)] * (a.ndim - 1) + [(0, width - a.shape[-1])])

    w_p = jnp.concatenate(group_cols(sxbc) + [gq, gk, gv, gg, dq, dk, dv,
                                              pad_cols(glr, LANES), pad_cols(sdt, LANES), sz],
                          axis=-1).astype(BF16)

    npair = GLA_HEADS // GLA_HP
    wg = jnp.zeros((depth, npair, LANES, 2 * GLA_QW), F32)
    bgs = []
    for hp in range(npair):
        cs = slice(hp * GLA_QW, (hp + 1) * GLA_QW)
        wg = wg.at[:, hp, 0:GLA_RANK, 0:GLA_QW].set(gla_wg_f[:, :, cs])
        wg = wg.at[:, hp, GLA_RANK:2 * GLA_RANK, GLA_QW:].set(gla_wg_b[:, :, cs])
        bgs.append(jnp.concatenate([gla_bg_f[:, cs], gla_bg_b[:, cs]], axis=-1))
    wg = wg.astype(BF16)
    bg = jnp.stack(bgs, axis=1)[:, :, None, :]
    gn = gla_norm[:, None, :]

    qn = jnp.tile(diff_qnorm, (1, 2))[:, None, :]
    kn = jnp.tile(diff_knorm, (1, 2))[:, None, :]
    shift = (1.01 * LOG2E * DIFF_DH ** 0.5 * jnp.max(jnp.abs(diff_qnorm), axis=-1)
             * jnp.max(jnp.abs(diff_knorm), axis=-1)).astype(F32)[:, None]
    lqk = jnp.stack([diff_lq1, diff_lk1, diff_lq2, diff_lk2], axis=1)
    sub = diff_subln[:, None, :]

    cw = jnp.concatenate(group_cols(ssd_conv_w), axis=-1).reshape(depth, SSD_CONV, SSD_GROUPS, SSD_GW)
    cw = cw.transpose(0, 2, 1, 3)
    cb = jnp.concatenate(group_cols(ssd_conv_b), axis=-1).reshape(depth, SSD_GROUPS, 1, SSD_GW)
    dtb = pad_cols(jnp.concatenate([ssd_dt_bias_f, ssd_dt_bias_b], axis=-1), LANES)[:, None, :]

    e_np = np.zeros((2, SSD_GROUPS, LANES, SSD_XW), np.float32)
    for d in range(2):
        for g in range(SSD_GROUPS):
            for e in range(SSD_E):
                e_np[d, g, d * SSD_HEADS + g * SSD_E + e, e * SSD_P:(e + 1) * SSD_P] = 1.0
    e_f = jnp.asarray(e_np[0], BF16)
    e_b = jnp.asarray(e_np[1], BF16)

    def per_lane(a):
        return jnp.repeat(a.reshape(depth, SSD_GROUPS, SSD_E), SSD_P, axis=-1)[:, :, None, :]

    a_f = per_lane(-jnp.exp(ssd_A_log_f))
    a_b = per_lane(-jnp.exp(ssd_A_log_b))
    dsk = per_lane(ssd_D)

    return dict(norm1=norm1[:, None, :], w_p=w_p, wg=wg, bg=bg, gn=gn, qn=qn, kn=kn, shift=shift, lqk=lqk, sub=sub,
                cw=cw, cb=cb, dtb=dtb, e_f=e_f, e_b=e_b, a_f=a_f, a_b=a_b, dsk=dsk,
                sn=ssd_norm[:, None, :], wo=w_out.astype(BF16), norm2=norm2[:, None, :],
                w1=w_mlp1.astype(BF16), w2=w_mlp2.astype(BF16))


def _rope_tables(seq):
    inv = ROPE_THETA ** (-jnp.arange(0, ROPE_DIM, 2, dtype=F32) / ROPE_DIM)
    ang = jnp.arange(seq, dtype=F32)[:, None] * inv[None, :]
    lane = np.arange(DIFF_DV) % DIFF_DH
    idx = lane % (ROPE_DIM // 2)
    in_rope = lane < ROPE_DIM
    sign = np.where(lane < ROPE_DIM // 2, -1.0, 1.0).astype(np.float32)
    cos_t = jnp.where(in_rope[None, :], jnp.cos(ang)[:, idx], 1.0)
    sin_t = jnp.where(in_rope[None, :], jnp.sin(ang)[:, idx] * sign[None, :], 0.0)
    return cos_t.astype(F32), sin_t.astype(F32)


def _trunk(x, pr):
    bsz, seq, _ = x.shape
    cos_t, sin_t = _rope_tables(seq)
    x2 = x.reshape(bsz * seq, D_MODEL)
    for l in range(DEPTH):
        p2 = _inproj(x2, pr["norm1"][l], pr["w_p"][l])
        p3 = p2.reshape(bsz, seq, P_WIDTH)
        o_gla = _gla(p3, pr["wg"][l], pr["bg"][l], pr["gn"][l])
        lambda_init = 0.8 - 0.6 * math.exp(-0.3 * l)
        o_diff = _diff(p3, cos_t, sin_t, pr["shift"][l], pr["qn"][l], pr["kn"][l], pr["lqk"][l], pr["sub"][l],
                       lambda_init)
        o_ssd = _ssd(p3, pr["cw"][l], pr["cb"][l], pr["dtb"][l], pr["e_f"], pr["e_b"],
                     pr["a_f"][l], pr["a_b"][l], pr["dsk"][l])
        x2 = _outmlp(x2, o_gla.reshape(bsz * seq, GLA_V), o_diff.reshape(bsz * seq, DIFF_V),
                     o_ssd.reshape(bsz * seq, SSD_INNER), p2, pr["wo"][l], pr["sn"][l], pr["norm2"][l],
                     pr["w1"][l], pr["w2"][l])
    return x2.reshape(bsz, seq, D_MODEL)


def kernel(x_prompt, x_sample, norm1, w_in, gla_wg_f, gla_bg_f, gla_wg_b, gla_bg_b, gla_norm, diff_qnorm, diff_knorm, diff_lq1, diff_lk1, diff_lq2, diff_lk2, diff_subln, ssd_conv_w, ssd_conv_b, ssd_dt_bias_f, ssd_dt_bias_b, ssd_A_log_f, ssd_A_log_b, ssd_D, ssd_norm, w_out, norm2, w_mlp1, w_mlp2):
    pr = _prep_params(norm1, w_in, gla_wg_f, gla_bg_f, gla_wg_b, gla_bg_b, gla_norm,
                      diff_qnorm, diff_knorm, diff_lq1, diff_lk1, diff_lq2, diff_lk2, diff_subln,
                      ssd_conv_w, ssd_conv_b, ssd_dt_bias_f, ssd_dt_bias_b, ssd_A_log_f, ssd_A_log_b,
                      ssd_D, ssd_norm, w_out, norm2, w_mlp1, w_mlp2)
    return (_trunk(x_prompt, pr), _trunk(x_sample, pr))
```

```python
import functools
import math

import numpy as np
import jax
import jax.numpy as jnp
from jax import lax
from jax.experimental import pallas as pl
from jax.experimental.pallas import tpu as pltpu

F32 = jnp.float32
BF16 = jnp.bfloat16

D_MODEL = 1024
DEPTH = 4
GLA_HEADS = 4
GLA_DK = 64
GLA_DV = 128
GLA_RANK = 16
GLA_GATE_NORM = 16.0
DIFF_HEADS = 4
DIFF_DH = 64
DIFF_DV = 2 * DIFF_DH
ROPE_THETA = 500000.0
ROPE_DIM = DIFF_DH // 4
SSD_HEADS = 8
SSD_P = 64
SSD_GROUPS = 2
SSD_E = SSD_HEADS // SSD_GROUPS
SSD_N = 64
SSD_CONV = 5
CHUNK = 64
D_FF = 4 * D_MODEL
EPS = 1e-6

GLA_QK = GLA_HEADS * GLA_DK
GLA_V = GLA_HEADS * GLA_DV
DIFF_QK = DIFF_HEADS * 2 * DIFF_DH
DIFF_V = DIFF_HEADS * DIFF_DV
SSD_INNER = SSD_HEADS * SSD_P
SSD_BC = SSD_GROUPS * SSD_N
SSD_XBC = SSD_INNER + 2 * SSD_BC
MIX_WIDTH = GLA_V + DIFF_V + SSD_INNER
IN_SIZES = (GLA_QK, GLA_QK, GLA_V, GLA_V, 2 * GLA_RANK,
            DIFF_QK, DIFF_QK, DIFF_V,
            SSD_INNER, SSD_XBC, 2 * SSD_HEADS)
IN_OFFSETS = tuple(int(v) for v in np.cumsum(IN_SIZES)[:-1])

LANES = 128

SSD_GW = SSD_E * SSD_P + 2 * SSD_N
C_SX = 0
C_GQ = C_SX + SSD_GROUPS * SSD_GW
C_GK = C_GQ + GLA_QK
C_GV = C_GK + GLA_QK
C_GG = C_GV + GLA_V
C_DQ = C_GG + GLA_V
C_DK = C_DQ + DIFF_QK
C_DV = C_DK + DIFF_QK
C_GLR = C_DV + DIFF_V
C_SDT = C_GLR + LANES
C_SZ = C_SDT + LANES
P_WIDTH = C_SZ + SSD_INNER

VMEM_LIMIT = 56 * 1024 * 1024
SCAN_UNROLL = 2
GROUP = 4
GRP_ROWS = GROUP * CHUNK
PAR_GROUPS = 4


def _cparams(sem):
    return pltpu.CompilerParams(dimension_semantics=sem, vmem_limit_bytes=VMEM_LIMIT)


def _dot(a, b):
    return jnp.dot(a, b, preferred_element_type=F32)


def _dot_nt(a, b):
    return lax.dot_general(a, b, (((1,), (1,)), ((), ())), preferred_element_type=F32)


def _dot_tn(a, b):
    return lax.dot_general(a, b, (((0,), (0,)), ((), ())), preferred_element_type=F32)


def _split(x):
    hi = x.astype(BF16)
    lo = (x - hi.astype(F32)).astype(BF16)
    return hi, lo


def _dot_exact_rhs(a01, x):
    hi, lo = _split(x)
    return _dot(a01, hi) + _dot(a01, lo)


def _dot_exact_lhs(x, b01):
    hi, lo = _split(x)
    return _dot(hi, b01) + _dot(lo, b01)


def _dot_select(sel01, x):
    hi = x.astype(BF16)
    r1 = x - hi.astype(F32)
    mid = r1.astype(BF16)
    lo = (r1 - mid.astype(F32)).astype(BF16)
    return _dot(sel01, hi) + _dot(sel01, mid) + _dot(sel01, lo)


def _silu(x):
    return x * (1.0 / (1.0 + jnp.exp(-x)))


def _softplus(x):
    return jnp.maximum(x, 0.0) + jnp.log(1.0 + jnp.exp(-jnp.abs(x)))


def _rms(x, g):
    ms = jnp.mean(x * x, axis=-1, keepdims=True)
    return x * lax.rsqrt(ms + EPS) * g


IN_TM = 512
IN_TN = 512


def _inproj_kernel(x_ref, g_ref, w_ref, o_ref):
    h = _rms(x_ref[...], g_ref[...]).astype(BF16)
    for j in range(P_WIDTH // IN_TN):
        o_ref[:, j * IN_TN:(j + 1) * IN_TN] = _dot(h, w_ref[:, j * IN_TN:(j + 1) * IN_TN]).astype(BF16)


def _inproj(x2, g, w):
    t = x2.shape[0]
    tm = min(IN_TM, t)
    return pl.pallas_call(
        _inproj_kernel,
        grid=(t // tm,),
        in_specs=[pl.BlockSpec((tm, D_MODEL), lambda i: (i, 0)),
                  pl.BlockSpec((1, D_MODEL), lambda i: (0, 0)),
                  pl.BlockSpec((D_MODEL, P_WIDTH), lambda i: (0, 0))],
        out_specs=pl.BlockSpec((tm, P_WIDTH), lambda i: (i, 0)),
        out_shape=jax.ShapeDtypeStruct((t, P_WIDTH), BF16),
        compiler_params=_cparams(("parallel",)),
        name="inproj",
    )(x2, g, w)


GLA_HP = 2
GLA_MAX_LOGDECAY = 60.0
GLA_QW = GLA_HP * GLA_DK
GLA_VW = GLA_HP * GLA_DV


def _gla_kernel(q_ref, k_ref, v_ref, gg_ref, glr_ref, wg_ref, bg_ref, gn_ref, o_ref,
                upd_ref, qe_ref, oacc_ref, d_ref, st_ref, *, seq):
    nc = seq // CHUNK
    ng = seq // GRP_ROWS
    scale = GLA_DK ** -0.5

    row = lax.broadcasted_iota(jnp.int32, (GRP_ROWS, GRP_ROWS), 0)
    col = lax.broadcasted_iota(jnp.int32, (GRP_ROWS, GRP_ROWS), 1)
    tri = ((row // CHUNK == col // CHUNK) & (col <= row)).astype(BF16)
    row2 = lax.broadcasted_iota(jnp.int32, (GLA_HP * GRP_ROWS, GRP_ROWS), 0) % GRP_ROWS
    col2 = lax.broadcasted_iota(jnp.int32, (GLA_HP * GRP_ROWS, GRP_ROWS), 1)
    same_chunk = row2 // CHUNK == col2 // CHUNK
    masks = (same_chunk & (col2 <= row2), same_chunk & (col2 >= row2))
    head0 = lax.broadcasted_iota(jnp.int32, (GRP_ROWS, GLA_QW), 1) < GLA_DK
    is_fwd = lax.broadcasted_iota(jnp.int32, (GRP_ROWS, 2 * GLA_QW), 1) < GLA_QW
    st_r = lax.broadcasted_iota(jnp.int32, (GLA_VW, GLA_QW), 0) // GLA_DV
    st_c = lax.broadcasted_iota(jnp.int32, (GLA_VW, GLA_QW), 1) // GLA_DK
    same_head = st_r == st_c

    n_par = PAR_GROUPS if ng % PAR_GROUPS == 0 else 1
    dirs = (slice(0, GLA_QW), slice(GLA_QW, 2 * GLA_QW))

    def gates(r):
        x = _dot(glr_ref[pl.ds(r, GRP_ROWS), :], wg_ref[...]) + bg_ref[...]
        return -_softplus(-x) * (1.0 / GLA_GATE_NORM)

    def decayed(r, lg, pre):
        tot = jnp.concatenate(
            [jnp.broadcast_to(pre[g * CHUNK + CHUNK - 1:(g + 1) * CHUNK, :], (CHUNK, 2 * GLA_QW))
             for g in range(GROUP)], axis=0)
        b = jnp.where(is_fwd, pre, tot - pre + lg)
        q = q_ref[pl.ds(r, GRP_ROWS), :].astype(F32)
        k = k_ref[pl.ds(r, GRP_ROWS), :].astype(F32)
        qe = jnp.concatenate([q, q], axis=1) * jnp.exp(b) * scale
        kk = jnp.concatenate([k, k], axis=1)
        ke = (kk * jnp.exp(-b)).astype(BF16)
        kl = (kk * jnp.exp(tot - b)).astype(BF16)
        q2 = [by_head(qe[:, sl]) for sl in dirs]
        return tot, qe.astype(BF16), ke, kl, q2, (b, q, k)

    def by_head(x):
        return jnp.concatenate([jnp.where(head0, x, 0.0), jnp.where(head0, 0.0, x)], axis=0).astype(BF16)

    def steep_scores(d, b, q, k):
        pi = row2 % CHUNK
        pj = col2 % CHUNK
        total = jnp.where(same_chunk & (pi == pj), _dot_nt(by_head(q * scale), k.astype(BF16)), 0.0)
        srow = row % CHUNK
        s = CHUNK // 2
        while s >= 1:
            blk_i, blk_j = pi // s, pj // s
            sib = same_chunk & (blk_i // 2 == blk_j // 2)
            if d == 0:
                pair = sib & (blk_i % 2 == 1) & (blk_j % 2 == 0)
                sel_q = (srow // s >= 1) & (col == row - srow % s - 1)
                sel_k = col == row - srow % s + s - 1
            else:
                pair = sib & (blk_i % 2 == 0) & (blk_j % 2 == 1)
                sel_q = (srow // s < CHUNK // s - 1) & (col == row - srow % s + s)
                sel_k = col == row - srow % s
            ref_q = _dot_select(sel_q.astype(BF16), b)
            ref_k = _dot_select(sel_k.astype(BF16), b)
            qt = q * jnp.exp(b - ref_q) * scale
            kt = (k * jnp.exp(ref_k - b)).astype(BF16)
            total = total + jnp.where(pair, _dot_nt(by_head(qt), kt), 0.0)
            s //= 2
        return total

    def phase_a(i, carry):
        ts = [i + j * (ng // n_par) for j in range(n_par)]
        rs = [pl.multiple_of(t * GRP_ROWS, GRP_ROWS) for t in ts]
        lgs = [gates(r) for r in rs]
        steep = functools.reduce(jnp.maximum, [jnp.max(-lg) for lg in lgs]) * CHUNK > GLA_MAX_LOGDECAY

        @pl.when(steep)
        def _():
            for t, r, lg in zip(ts, rs, lgs):
                phase_a_rest([t], [r], [lg], True)

        @pl.when(jnp.logical_not(steep))
        def _():
            phase_a_rest(ts, rs, lgs, False)

        return carry

    def phase_a_rest(ts, rs, lgs, steep):
        pres = [_dot_exact_rhs(tri, lg) for lg in lgs]
        dec = [decayed(r, lg, pre) for r, lg, pre in zip(rs, lgs, pres)]
        vs = [v_ref[pl.ds(r, GRP_ROWS), :] for r in rs]
        if steep:
            scores = [[steep_scores(d, b[:, dirs[d]], q, k) for d in range(2)] for (*_, (b, q, k)) in dec]
        else:
            scores = [[_dot_nt(q2[d], ke[:, dirs[d]]) for d in range(2)] for (_, _, ke, _, q2, _) in dec]
        upds = [[[_dot_tn(v[g * CHUNK:(g + 1) * CHUNK], kl[g * CHUNK:(g + 1) * CHUNK, dirs[d]])
                  for g in range(GROUP)] for d in range(2)] for v, (_, _, _, kl, _, _) in zip(vs, dec)]
        for t, r, v, (tot, qe, _, _, _, _), sc, up in zip(ts, rs, vs, dec, scores, upds):
            qe_ref[pl.ds(r, GRP_ROWS), :] = qe
            o_sum = None
            for d in range(2):
                a = jnp.where(masks[d], sc[d], 0.0).astype(BF16)
                o_d = jnp.concatenate(
                    [_dot(a[h * GRP_ROWS:(h + 1) * GRP_ROWS], v[:, h * GLA_DV:(h + 1) * GLA_DV])
                     for h in range(GLA_HP)], axis=1)
                o_sum = o_d if o_sum is None else o_sum + o_d
                for g in range(GROUP):
                    upd_ref[d * nc + t * GROUP + g] = jnp.where(same_head, up[d][g], 0.0).astype(BF16)
            oacc_ref[pl.ds(r, GRP_ROWS), :] = o_sum
            for g in range(GROUP):
                d_ref[t * GROUP + g] = jnp.exp(tot[g * CHUNK:g * CHUNK + 8, :])

    lax.fori_loop(0, ng // n_par, phase_a, 0)

    st_ref[...] = jnp.zeros_like(st_ref)

    def phase_b(c, carry):
        cb = nc - 1 - c
        for d, idx, cd in ((0, c, c), (1, nc + cb, cb)):
            dec = d_ref[cd][0:1, d * GLA_QW:(d + 1) * GLA_QW]
            u = upd_ref[idx].astype(F32)
            s = st_ref[d]
            upd_ref[idx] = s.astype(BF16)
            st_ref[d] = s * dec + u
        return carry

    lax.fori_loop(0, nc, phase_b, 0, unroll=SCAN_UNROLL)

    def phase_c(i, carry):
        ts = [i + j * (ng // n_par) for j in range(n_par)]
        rs = [pl.multiple_of(t * GRP_ROWS, GRP_ROWS) for t in ts]
        inter = []
        for t, r in zip(ts, rs):
            parts = []
            for g in range(GROUP):
                qe = qe_ref[pl.ds(r + g * CHUNK, CHUNK), :]
                ods = [_dot_nt(qe[:, dirs[d]], upd_ref[d * nc + t * GROUP + g].astype(BF16)) for d in range(2)]
                parts.append(ods[0] + ods[1])
            inter.append(parts)
        for r, parts in zip(rs, inter):
            o = oacc_ref[pl.ds(r, GRP_ROWS), :] + jnp.concatenate(parts, axis=0)
            gg = gg_ref[pl.ds(r, GRP_ROWS), :].astype(F32)
            normed = [_rms(o[:, h * GLA_DV:(h + 1) * GLA_DV], gn_ref[...]) for h in range(GLA_HP)]
            y = jnp.concatenate(normed, axis=1) * _silu(gg)
            o_ref[pl.ds(r, GRP_ROWS), :] = y.astype(o_ref.dtype)
        return carry

    lax.fori_loop(0, ng // n_par, phase_c, 0)


def _gla(p3, wg, bg, gn):
    bsz, seq, _ = p3.shape
    assert seq % GRP_ROWS == 0
    nc = seq // CHUNK
    npair = GLA_HEADS // GLA_HP
    return pl.pallas_call(
        functools.partial(_gla_kernel, seq=seq),
        grid=(bsz, npair),
        in_specs=[pl.BlockSpec((None, seq, GLA_QW), lambda b, h: (b, 0, C_GQ // GLA_QW + h)),
                  pl.BlockSpec((None, seq, GLA_QW), lambda b, h: (b, 0, C_GK // GLA_QW + h)),
                  pl.BlockSpec((None, seq, GLA_VW), lambda b, h: (b, 0, C_GV // GLA_VW + h)),
                  pl.BlockSpec((None, seq, GLA_VW), lambda b, h: (b, 0, C_GG // GLA_VW + h)),
                  pl.BlockSpec((None, seq, LANES), lambda b, h: (b, 0, C_GLR // LANES)),
                  pl.BlockSpec((None, LANES, 2 * GLA_QW), lambda b, h: (h, 0, 0)),
                  pl.BlockSpec((None, 1, 2 * GLA_QW), lambda b, h: (h, 0, 0)),
                  pl.BlockSpec((1, GLA_DV), lambda b, h: (0, 0))],
        out_specs=pl.BlockSpec((None, seq, GLA_VW), lambda b, h: (b, 0, h)),
        out_shape=jax.ShapeDtypeStruct((bsz, seq, GLA_V), BF16),
        scratch_shapes=[pltpu.VMEM((2 * nc, GLA_VW, GLA_QW), BF16),
                        pltpu.VMEM((seq, 2 * GLA_QW), BF16),
                        pltpu.VMEM((seq, GLA_VW), F32),
                        pltpu.VMEM((nc, 8, 2 * GLA_QW), F32),
                        pltpu.VMEM((2, GLA_VW, GLA_QW), F32)],
        compiler_params=_cparams(("parallel", "parallel")),
        name="gla",
    )(p3, p3, p3, p3, p3, wg, bg, gn)


DIFF_QB = 512
DIFF_NSUB = 4
DIFF_TK = 256
DIFF_PREP_ROWS = 256
DIFF_PREP_TILES = 4
LOG2E = math.log2(math.e)
DIFF_MAX_SHIFT = 60.0


def _diff_kernel(shift_ref, q_ref, k_ref, v_ref, cos_ref, sin_ref, qn_ref, kn_ref, lqk_ref, sub_ref, o_ref,
                 qp_ref, kp_ref, *, seq, lambda_init):
    qi = pl.program_id(2)
    lane = lax.broadcasted_iota(jnp.int32, (1, DIFF_DV), 1)
    comp0 = lane < DIFF_DH

    @pl.when(qi == 0)
    def _():
        rows = min(DIFF_PREP_ROWS, seq)
        src = lax.broadcasted_iota(jnp.int32, (DIFF_DV, DIFF_DV), 0)
        dst = lax.broadcasted_iota(jnp.int32, (DIFF_DV, DIFF_DV), 1)
        same_comp = (src // DIFF_DH == dst // DIFF_DH).astype(BF16)
        half = ROPE_DIM // 2
        dpos = dst % DIFF_DH
        partner_of = ((dpos < half) & (src == dst + half)) | ((dpos >= half) & (dpos < ROPE_DIM) & (src == dst - half))
        partner_of = partner_of.astype(BF16)

        tiles = DIFF_PREP_TILES if (seq // rows) % DIFF_PREP_TILES == 0 else 1
        work = ((q_ref, qn_ref, qp_ref, DIFF_DH ** -0.5 * LOG2E), (k_ref, kn_ref, kp_ref, 1.0))

        def body(i, carry):
            rs = [pl.multiple_of((i * tiles + j) * rows, rows) for j in range(tiles)]
            items = [(src_ref, g_ref, dst_ref, mul, r) for r in rs for (src_ref, g_ref, dst_ref, mul) in work]
            xs = [src_ref[pl.ds(r, rows), :].astype(F32) for (src_ref, _, _, _, r) in items]
            ms = [_dot((x * x).astype(BF16), same_comp) * (1.0 / DIFF_DH) for x in xs]
            xn = [x * lax.rsqrt(m + EPS) * it[1][...] for x, m, it in zip(xs, ms, items)]
            partner = [_dot(v.astype(BF16), partner_of) for v in xn]
            for v, p, (_, _, dst_ref, mul, r) in zip(xn, partner, items):
                out = v * cos_ref[pl.ds(r, rows), :] + p * sin_ref[pl.ds(r, rows), :]
                dst_ref[pl.ds(r, rows), :] = (out * mul).astype(BF16)
            return carry

        lax.fori_loop(0, seq // (rows * tiles), body, 0)

    lqk = lqk_ref[...]
    l1 = jnp.sum(lqk[0:1] * lqk[1:2], axis=-1, keepdims=True)
    l2 = jnp.sum(lqk[2:3] * lqk[3:4], axis=-1, keepdims=True)
    lam = jnp.exp(l1) - jnp.exp(l2) + lambda_init

    def stacked_q(r0, n):
        qb = qp_ref[pl.ds(r0, n), :]
        zero = jnp.zeros_like(qb)
        return jnp.concatenate([jnp.where(comp0, qb, zero), jnp.where(comp0, zero, qb)], axis=0)

    def finish(rows, n, ov, l):
        rinv = 1.0 / l
        o = ov[0:n] * rinv[0:n] - ov[n:] * (lam * rinv[n:])
        o = _rms(o, sub_ref[...]) * (1.0 - lambda_init)
        o_ref[rows, :] = o.astype(o_ref.dtype)

    shift = shift_ref[0]

    @pl.when(shift <= DIFF_MAX_SHIFT)
    def _():
        qz = stacked_q(pl.multiple_of(qi * DIFF_QB, DIFF_QB), DIFF_QB)
        nkt = seq // DIFF_TK
        lacc = jnp.zeros((2 * DIFF_QB, LANES), F32)
        ov = jnp.zeros((2 * DIFF_QB, DIFF_DV), F32)
        s_prev = None
        for j in range(nkt + 1):
            s_cur = _dot_nt(qz, kp_ref[j * DIFF_TK:(j + 1) * DIFF_TK, :]) if j < nkt else None
            if s_prev is not None:
                e = jnp.exp2(s_prev - shift)
                for c in range(DIFF_TK // LANES):
                    lacc = lacc + e[:, c * LANES:(c + 1) * LANES]
                ov = ov + _dot(e.astype(BF16), v_ref[(j - 1) * DIFF_TK:j * DIFF_TK, :])
            s_prev = s_cur
        finish(slice(0, DIFF_QB), DIFF_QB, ov, jnp.sum(lacc, axis=-1, keepdims=True))

    @pl.when(shift > DIFF_MAX_SHIFT)
    def _():
        sb = DIFF_QB // DIFF_NSUB
        scores = [_dot_nt(stacked_q(pl.multiple_of(qi * DIFF_QB + j * sb, sb), sb), kp_ref[...])
                  for j in range(DIFF_NSUB)]
        for j, s in enumerate(scores):
            e = jnp.exp2(s - jnp.max(s, axis=-1, keepdims=True))
            finish(slice(j * sb, (j + 1) * sb), sb, _dot(e.astype(BF16), v_ref[...]),
                   jnp.sum(e, axis=-1, keepdims=True))


def _diff(p3, cos_t, sin_t, shift, qn, kn, lqk, sub, lambda_init):
    bsz, seq, _ = p3.shape
    qb = DIFF_QB
    assert seq % qb == 0 and seq % DIFF_TK == 0
    return pl.pallas_call(
        functools.partial(_diff_kernel, seq=seq, lambda_init=lambda_init),
        grid=(bsz, DIFF_HEADS, seq // qb),
        in_specs=[pl.BlockSpec(memory_space=pltpu.SMEM),
                  pl.BlockSpec((None, seq, DIFF_DV), lambda b, h, i: (b, 0, C_DQ // DIFF_DV + h)),
                  pl.BlockSpec((None, seq, DIFF_DV), lambda b, h, i: (b, 0, C_DK // DIFF_DV + h)),
                  pl.BlockSpec((None, seq, DIFF_DV), lambda b, h, i: (b, 0, C_DV // DIFF_DV + h)),
                  pl.BlockSpec((seq, DIFF_DV), lambda b, h, i: (0, 0)),
                  pl.BlockSpec((seq, DIFF_DV), lambda b, h, i: (0, 0)),
                  pl.BlockSpec((1, DIFF_DV), lambda b, h, i: (0, 0)),
                  pl.BlockSpec((1, DIFF_DV), lambda b, h, i: (0, 0)),
                  pl.BlockSpec((4, DIFF_DH), lambda b, h, i: (0, 0)),
                  pl.BlockSpec((1, DIFF_DV), lambda b, h, i: (0, 0))],
        out_specs=pl.BlockSpec((None, qb, DIFF_DV), lambda b, h, i: (b, i, h)),
        out_shape=jax.ShapeDtypeStruct((bsz, seq, DIFF_V), BF16),
        scratch_shapes=[pltpu.VMEM((seq, DIFF_DV), BF16),
                        pltpu.VMEM((seq, DIFF_DV), BF16)],
        compiler_params=_cparams(("parallel", "parallel", "arbitrary")),
        name="diffattn",
    )(shift, p3, p3, p3, cos_t, sin_t, qn, kn, lqk, sub)


SSD_XW = SSD_E * SSD_P
SSD_HALO = 8
SSD_CONV_ROWS = 256
SSD_PAR_GROUPS = 4


def _ssd_kernel(xbc_ref, sdt_ref, cw_ref, cb_ref, dtb_ref, ef_ref, eb_ref, af_ref, ab_ref, dsk_ref, o_ref,
                xpad_ref, xc_ref, dt_ref, yacc_ref, upd_ref, eac_ref, d_ref, st_ref, *, seq):
    nc = seq // CHUNK
    ng = seq // GRP_ROWS
    rows = min(SSD_CONV_ROWS, seq)
    pad = SSD_CONV // 2

    xpad_ref[0:SSD_HALO, :] = jnp.zeros((SSD_HALO, SSD_GW), F32)
    xpad_ref[seq + SSD_HALO:seq + 2 * SSD_HALO, :] = jnp.zeros((SSD_HALO, SSD_GW), F32)

    def copy_body(i, carry):
        r = pl.multiple_of(i * rows, rows)
        xpad_ref[pl.ds(r + SSD_HALO, rows), :] = xbc_ref[pl.ds(r, rows), :].astype(F32)
        dt_ref[pl.ds(r, rows), :] = _softplus(sdt_ref[pl.ds(r, rows), :].astype(F32) + dtb_ref[...])
        return carry

    lax.fori_loop(0, seq // rows, copy_body, 0)

    def conv_body(i, carry):
        r = pl.multiple_of(i * rows, rows)
        win = xpad_ref[pl.ds(r, rows + 2 * SSD_HALO), :]
        acc = cb_ref[...] + win[SSD_HALO - pad:SSD_HALO - pad + rows] * cw_ref[0:1, :]
        for w in range(1, SSD_CONV):
            acc = acc + win[SSD_HALO - pad + w:SSD_HALO - pad + w + rows] * cw_ref[w:w + 1, :]
        xc_ref[pl.ds(r, rows), :] = _silu(acc)
        return carry

    lax.fori_loop(0, seq // rows, conv_body, 0)
    row = lax.broadcasted_iota(jnp.int32, (GRP_ROWS, GRP_ROWS), 0)
    col = lax.broadcasted_iota(jnp.int32, (GRP_ROWS, GRP_ROWS), 1)
    same_chunk = row // CHUNK == col // CHUNK
    tri = (same_chunk & (col <= row)).astype(BF16)
    blk = same_chunk.astype(BF16)
    rowc = lax.broadcasted_iota(jnp.int32, (GRP_ROWS, SSD_XW), 0) % CHUNK
    colj = lax.broadcasted_iota(jnp.int32, (GRP_ROWS, SSD_XW), 1) % CHUNK
    le = colj <= rowc
    ge = colj >= rowc
    mask_ij = (le, ge)
    mask_sum = (ge, le)
    bd_r = lax.broadcasted_iota(jnp.int32, (SSD_E * CHUNK, SSD_XW), 0) // CHUNK
    bd_c = lax.broadcasted_iota(jnp.int32, (SSD_E * CHUNK, SSD_XW), 1) // SSD_P
    same_head = bd_r == bd_c
    e_refs = (ef_ref, eb_ref)
    a_refs = (af_ref, ab_ref)
    dirs = (slice(0, SSD_XW), slice(SSD_XW, 2 * SSD_XW))
    n_par = SSD_PAR_GROUPS if ng % SSD_PAR_GROUPS == 0 else 1

    def chunk_rows(g):
        return slice(g * CHUNK, (g + 1) * CHUNK)

    def decays(r, d):
        dt4 = _dot_exact_lhs(dt_ref[pl.ds(r, GRP_ROWS), :], e_refs[d][...])
        la4 = dt4 * a_refs[d][...]
        pre = _dot_exact_rhs(tri, la4)
        asum = _dot_exact_rhs(blk, jnp.where(mask_sum[d], la4, 0.0))
        return dt4, la4, pre, asum

    def phase_a(i, carry):
        ts = [i + j * (ng // n_par) for j in range(n_par)]
        rs = [pl.multiple_of(t * GRP_ROWS, GRP_ROWS) for t in ts]
        dec = [[decays(r, d) for d in range(2)] for r in rs]
        xs = [xc_ref[pl.ds(r, GRP_ROWS), 0:SSD_XW] for r in rs]
        bms = [xc_ref[pl.ds(r, GRP_ROWS), SSD_XW:SSD_XW + SSD_N].astype(BF16) for r in rs]
        cms = [xc_ref[pl.ds(r, GRP_ROWS), SSD_XW + SSD_N:SSD_XW + 2 * SSD_N].astype(BF16) for r in rs]
        cbs = [jnp.concatenate([_dot_nt(cm[chunk_rows(g)], jnp.concatenate([bm[chunk_rows(g)]] * SSD_E, axis=0))
                                for g in range(GROUP)], axis=0) for bm, cm in zip(bms, cms)]
        stage = []
        for x, cb4, dd in zip(xs, cbs, dec):
            per_dir = []
            for d in range(2):
                dt4, la4, pre, asum = dd[d]
                tot = jnp.concatenate(
                    [jnp.broadcast_to(pre[g * CHUNK + CHUNK - 1:(g + 1) * CHUNK, :], (CHUNK, SSD_XW))
                     for g in range(GROUP)], axis=0)
                acol = pre if d == 0 else tot - pre + la4
                lm = jnp.exp(jnp.where(mask_ij[d], acol - asum, -jnp.inf))
                m4 = (cb4 * lm).astype(BF16)
                xd = x * dt4
                wx = (xd * jnp.exp(tot - acol)).astype(BF16)
                per_dir.append((tot, acol, m4, xd, wx))
            stage.append(per_dir)
        for t, r, x, bm, per_dir in zip(ts, rs, xs, bms, stage):
            y = dsk_ref[...] * x
            for d in range(2):
                tot, acol, m4, xd, wx = per_dir[d]
                parts = []
                for g in range(GROUP):
                    cr = chunk_rows(g)
                    xbd = jnp.where(same_head, jnp.concatenate([xd[cr]] * SSD_E, axis=0), 0.0).astype(BF16)
                    parts.append(_dot(m4[cr], xbd))
                    upd_ref[d * nc + t * GROUP + g] = _dot_tn(bm[cr], wx[cr]).astype(BF16)
                y = y + jnp.concatenate(parts, axis=0)
                eac_ref[pl.ds(r, GRP_ROWS), dirs[d]] = jnp.exp(acol).astype(BF16)
                for g in range(GROUP):
                    d_ref[t * GROUP + g, :, dirs[d]] = jnp.exp(tot[g * CHUNK:g * CHUNK + 8, :])
            yacc_ref[pl.ds(r, GRP_ROWS), :] = y
        return carry

    lax.fori_loop(0, ng // n_par, phase_a, 0)

    st_ref[...] = jnp.zeros_like(st_ref)

    def phase_b(c, carry):
        cb = nc - 1 - c
        for d, idx, cd in ((0, c, c), (1, nc + cb, cb)):
            decay = d_ref[cd][0:1, dirs[d]]
            u = upd_ref[idx].astype(F32)
            s = st_ref[d]
            upd_ref[idx] = s.astype(BF16)
            st_ref[d] = s * decay + u
        return carry

    lax.fori_loop(0, nc, phase_b, 0, unroll=SCAN_UNROLL)

    def phase_c(i, carry):
        ts = [i + j * (ng // n_par) for j in range(n_par)]
        rs = [pl.multiple_of(t * GRP_ROWS, GRP_ROWS) for t in ts]
        inter = []
        for t, r in zip(ts, rs):
            cm = xc_ref[pl.ds(r, GRP_ROWS), SSD_XW + SSD_N:SSD_XW + 2 * SSD_N].astype(BF16)
            per_dir = []
            for d in range(2):
                per_dir.append(jnp.concatenate(
                    [_dot(cm[chunk_rows(g)], upd_ref[d * nc + t * GROUP + g].astype(BF16)) for g in range(GROUP)],
                    axis=0))
            inter.append(per_dir)
        for r, per_dir in zip(rs, inter):
            y = yacc_ref[pl.ds(r, GRP_ROWS), :]
            for d in range(2):
                y = y + per_dir[d] * eac_ref[pl.ds(r, GRP_ROWS), dirs[d]].astype(F32)
            o_ref[pl.ds(r, GRP_ROWS), :] = y.astype(o_ref.dtype)
        return carry

    lax.fori_loop(0, ng // n_par, phase_c, 0)


def _ssd(p3, cw, cb, dtb, ef, eb, af, ab, dsk):
    bsz, seq, _ = p3.shape
    assert seq % GRP_ROWS == 0
    nc = seq // CHUNK
    g3 = lambda shape: pl.BlockSpec((None,) + shape, lambda b, g: (g, 0, 0))
    return pl.pallas_call(
        functools.partial(_ssd_kernel, seq=seq),
        grid=(bsz, SSD_GROUPS),
        in_specs=[pl.BlockSpec((None, seq, SSD_GW), lambda b, g: (b, 0, C_SX // SSD_GW + g)),
                  pl.BlockSpec((None, seq, LANES), lambda b, g: (b, 0, C_SDT // LANES)),
                  g3((SSD_CONV, SSD_GW)), g3((1, SSD_GW)),
                  pl.BlockSpec((1, LANES), lambda b, g: (0, 0)),
                  g3((LANES, SSD_XW)), g3((LANES, SSD_XW)),
                  g3((1, SSD_XW)), g3((1, SSD_XW)), g3((1, SSD_XW))],
        out_specs=pl.BlockSpec((None, seq, SSD_XW), lambda b, g: (b, 0, g)),
        out_shape=jax.ShapeDtypeStruct((bsz, seq, SSD_INNER), BF16),
        scratch_shapes=[pltpu.VMEM((seq + 2 * SSD_HALO, SSD_GW), F32),
                        pltpu.VMEM((seq, SSD_GW), F32),
                        pltpu.VMEM((seq, LANES), F32),
                        pltpu.VMEM((seq, SSD_XW), F32),
                        pltpu.VMEM((2 * nc, SSD_N, SSD_XW), BF16),
                        pltpu.VMEM((seq, 2 * SSD_XW), BF16),
                        pltpu.VMEM((nc, 8, 2 * SSD_XW), F32),
                        pltpu.VMEM((2, SSD_N, SSD_XW), F32)],
        compiler_params=_cparams(("parallel", "parallel")),
        name="ssd",
    )(p3, p3, cw, cb, dtb, ef, eb, af, ab, dsk)


MLP_TM = 512
MLP_TF = 1024


def _outmlp_kernel(x_ref, gla_ref, diff_ref, ssd_ref, z_ref, wo_ref, sn_ref, g2_ref, w1_ref, w2_ref, o_ref):
    t = ssd_ref[...].astype(F32) * _silu(z_ref[...].astype(F32))
    sn = _rms(t, sn_ref[...]).astype(BF16)
    xn = (x_ref[...]
          + _dot(gla_ref[...], wo_ref[0:GLA_V, :])
          + _dot(diff_ref[...], wo_ref[GLA_V:GLA_V + DIFF_V, :])
          + _dot(sn, wo_ref[GLA_V + DIFF_V:MIX_WIDTH, :]))
    o_ref[...] = xn
    h = _rms(xn, g2_ref[...]).astype(BF16)
    for k in range(D_FF // MLP_TF):
        a = _dot(h, w1_ref[:, k * MLP_TF:(k + 1) * MLP_TF])
        a = jnp.square(jnp.maximum(a, 0.0)).astype(BF16)
        o_ref[...] += _dot(a, w2_ref[k * MLP_TF:(k + 1) * MLP_TF, :])


def _outmlp(x2, gla2, diff2, ssd2, p2, wo, sn, g2, w1, w2):
    t = x2.shape[0]
    tm = min(MLP_TM, t)
    row = lambda w: pl.BlockSpec((tm, w), lambda i: (i, 0))
    resident = lambda shape: pl.BlockSpec(shape, lambda i: (0, 0), pipeline_mode=pl.Buffered(1))
    return pl.pallas_call(
        _outmlp_kernel,
        grid=(t // tm,),
        in_specs=[row(D_MODEL), row(GLA_V), row(DIFF_V), row(SSD_INNER),
                  pl.BlockSpec((tm, SSD_INNER), lambda i: (i, C_SZ // SSD_INNER)),
                  resident((MIX_WIDTH, D_MODEL)),
                  pl.BlockSpec((1, SSD_INNER), lambda i: (0, 0)),
                  pl.BlockSpec((1, D_MODEL), lambda i: (0, 0)),
                  resident((D_MODEL, D_FF)),
                  resident((D_FF, D_MODEL))],
        out_specs=pl.BlockSpec((tm, D_MODEL), lambda i: (i, 0)),
        out_shape=jax.ShapeDtypeStruct((t, D_MODEL), F32),
        compiler_params=_cparams(("parallel",)),
        name="outmlp",
    )(x2, gla2, diff2, ssd2, p2, wo, sn, g2, w1, w2)


def _prep_params(norm1, w_in, gla_wg_f, gla_bg_f, gla_wg_b, gla_bg_b, gla_norm,
                 diff_qnorm, diff_knorm, diff_lq1, diff_lk1, diff_lq2, diff_lk2, diff_subln,
                 ssd_conv_w, ssd_conv_b, ssd_dt_bias_f, ssd_dt_bias_b, ssd_A_log_f, ssd_A_log_b,
                 ssd_D, ssd_norm, w_out, norm2, w_mlp1, w_mlp2):
    depth = w_in.shape[0]
    (gq, gk, gv, gg, glr, dq, dk, dv, sz, sxbc, sdt) = jnp.split(w_in, IN_OFFSETS, axis=-1)

    def group_cols(a):
        xs, bm, cm = a[..., :SSD_INNER], a[..., SSD_INNER:SSD_INNER + SSD_BC], a[..., SSD_INNER + SSD_BC:]
        parts = []
        for g in range(SSD_GROUPS):
            parts += [xs[..., g * SSD_XW:(g + 1) * SSD_XW], bm[..., g * SSD_N:(g + 1) * SSD_N],
                      cm[..., g * SSD_N:(g + 1) * SSD_N]]
        return parts

    def pad_cols(a, width):
        return jnp.pad(a, [(0, 0)] * (a.ndim - 1) + [(0, width - a.shape[-1])])

    w_p = jnp.concatenate(group_cols(sxbc) + [gq, gk, gv, gg, dq, dk, dv,
                                              pad_cols(glr, LANES), pad_cols(sdt, LANES), sz],
                          axis=-1).astype(BF16)

    npair = GLA_HEADS // GLA_HP
    wg = jnp.zeros((depth, npair, LANES, 2 * GLA_QW), F32)
    bgs = []
    for hp in range(npair):
        cs = slice(hp * GLA_QW, (hp + 1) * GLA_QW)
        wg = wg.at[:, hp, 0:GLA_RANK, 0:GLA_QW].set(gla_wg_f[:, :, cs])
        wg = wg.at[:, hp, GLA_RANK:2 * GLA_RANK, GLA_QW:].set(gla_wg_b[:, :, cs])
        bgs.append(jnp.concatenate([gla_bg_f[:, cs], gla_bg_b[:, cs]], axis=-1))
    wg = wg.astype(BF16)
    bg = jnp.stack(bgs, axis=1)[:, :, None, :]
    gn = gla_norm[:, None, :]

    qn = jnp.tile(diff_qnorm, (1, 2))[:, None, :]
    kn = jnp.tile(diff_knorm, (1, 2))[:, None, :]
    shift = (1.01 * LOG2E * DIFF_DH ** 0.5 * jnp.max(jnp.abs(diff_qnorm), axis=-1)
             * jnp.max(jnp.abs(diff_knorm), axis=-1)).astype(F32)[:, None]
    lqk = jnp.stack([diff_lq1, diff_lk1, diff_lq2, diff_lk2], axis=1)
    sub = diff_subln[:, None, :]

    cw = jnp.concatenate(group_cols(ssd_conv_w), axis=-1).reshape(depth, SSD_CONV, SSD_GROUPS, SSD_GW)
    cw = cw.transpose(0, 2, 1, 3)
    cb = jnp.concatenate(group_cols(ssd_conv_b), axis=-1).reshape(depth, SSD_GROUPS, 1, SSD_GW)
    dtb = pad_cols(jnp.concatenate([ssd_dt_bias_f, ssd_dt_bias_b], axis=-1), LANES)[:, None, :]

    e_np = np.zeros((2, SSD_GROUPS, LANES, SSD_XW), np.float32)
    for d in range(2):
        for g in range(SSD_GROUPS):
            for e in range(SSD_E):
                e_np[d, g, d * SSD_HEADS + g * SSD_E + e, e * SSD_P:(e + 1) * SSD_P] = 1.0
    e_f = jnp.asarray(e_np[0], BF16)
    e_b = jnp.asarray(e_np[1], BF16)

    def per_lane(a):
        return jnp.repeat(a.reshape(depth, SSD_GROUPS, SSD_E), SSD_P, axis=-1)[:, :, None, :]

    a_f = per_lane(-jnp.exp(ssd_A_log_f))
    a_b = per_lane(-jnp.exp(ssd_A_log_b))
    dsk = per_lane(ssd_D)

    return dict(norm1=norm1[:, None, :], w_p=w_p, wg=wg, bg=bg, gn=gn, qn=qn, kn=kn, shift=shift, lqk=lqk, sub=sub,
                cw=cw, cb=cb, dtb=dtb, e_f=e_f, e_b=e_b, a_f=a_f, a_b=a_b, dsk=dsk,
                sn=ssd_norm[:, None, :], wo=w_out.astype(BF16), norm2=norm2[:, None, :],
                w1=w_mlp1.astype(BF16), w2=w_mlp2.astype(BF16))


def _rope_tables(seq):
    inv = ROPE_THETA ** (-jnp.arange(0, ROPE_DIM, 2, dtype=F32) / ROPE_DIM)
    ang = jnp.arange(seq, dtype=F32)[:, None] * inv[None, :]
    lane = np.arange(DIFF_DV) % DIFF_DH
    idx = lane % (ROPE_DIM // 2)
    in_rope = lane < ROPE_DIM
    sign = np.where(lane < ROPE_DIM // 2, -1.0, 1.0).astype(np.float32)
    cos_t = jnp.where(in_rope[None, :], jnp.cos(ang)[:, idx], 1.0)
    sin_t = jnp.where(in_rope[None, :], jnp.sin(ang)[:, idx] * sign[None, :], 0.0)
    return cos_t.astype(F32), sin_t.astype(F32)


def _trunk(x, pr):
    bsz, seq, _ = x.shape
    cos_t, sin_t = _rope_tables(seq)
    x2 = x.reshape(bsz * seq, D_MODEL)
    for l in range(DEPTH):
        p2 = _inproj(x2, pr["norm1"][l], pr["w_p"][l])
        p3 = p2.reshape(bsz, seq, P_WIDTH)
        o_gla = _gla(p3, pr["wg"][l], pr["bg"][l], pr["gn"][l])
        lambda_init = 0.8 - 0.6 * math.exp(-0.3 * l)
        o_diff = _diff(p3, cos_t, sin_t, pr["shift"][l], pr["qn"][l], pr["kn"][l], pr["lqk"][l], pr["sub"][l],
                       lambda_init)
        o_ssd = _ssd(p3, pr["cw"][l], pr["cb"][l], pr["dtb"][l], pr["e_f"], pr["e_b"],
                     pr["a_f"][l], pr["a_b"][l], pr["dsk"][l])
        x2 = _outmlp(x2, o_gla.reshape(bsz * seq, GLA_V), o_diff.reshape(bsz * seq, DIFF_V),
                     o_ssd.reshape(bsz * seq, SSD_INNER), p2, pr["wo"][l], pr["sn"][l], pr["norm2"][l],
                     pr["w1"][l], pr["w2"][l])
    return x2.reshape(bsz, seq, D_MODEL)


def kernel(x_prompt, x_sample, norm1, w_in, gla_wg_f, gla_bg_f, gla_wg_b, gla_bg_b, gla_norm, diff_qnorm, diff_knorm, diff_lq1, diff_lk1, diff_lq2, diff_lk2, diff_subln, ssd_conv_w, ssd_conv_b, ssd_dt_bias_f, ssd_dt_bias_b, ssd_A_log_f, ssd_A_log_b, ssd_D, ssd_norm, w_out, norm2, w_mlp1, w_mlp2):
    pr = _prep_params(norm1, w_in, gla_wg_f, gla_bg_f, gla_wg_b, gla_bg_b, gla_norm,
                      diff_qnorm, diff_knorm, diff_lq1, diff_lk1, diff_lq2, diff_lk2, diff_subln,
                      ssd_conv_w, ssd_conv_b, ssd_dt_bias_f, ssd_dt_bias_b, ssd_A_log_f, ssd_A_log_b,
                      ssd_D, ssd_norm, w_out, norm2, w_mlp1, w_mlp2)
    return (_trunk(x_prompt, pr), _trunk(x_sample, pr))
```
